```python
import math
import jax, jax.numpy as jnp
from jax import lax
import numpy as np

D_MODEL = 4096
BATCH = 2
SEQ = 8192
DEPTH = 2

CHUNK = 64
NORM_EPS = 1e-6
D_FF = 4 * D_MODEL

A_HEADS = D_MODEL // 256
A_DK = 128
A_DV = 128
A_KW = A_HEADS * A_DK
A_VW = A_HEADS * A_DV
A_CONV_W = 2 * A_KW + A_VW
CONV_K = 4

B_HEADS = D_MODEL // 512
B_DK = 256
B_DV = 256
B_KW = B_HEADS * B_DK
B_VW = B_HEADS * B_DV
ROPE_BASE = 10000.0

C_HEADS = 4
C_KW = D_MODEL // 2
C_VW = D_MODEL
C_DK = C_KW // C_HEADS
C_DV = C_VW // C_HEADS
GK_RANK = 16
GK_NORMALIZER = 16.0

AB_SPLITS = (A_CONV_W, A_VW, A_HEADS, A_HEADS, B_KW, B_KW, B_VW, B_VW)
IN_AB = sum(AB_SPLITS)
C_SPLITS = (C_KW, C_KW, C_VW, C_VW)
IN_C = sum(C_SPLITS)

kernel_name = "hybrid_deltanet_retention_gla_trunk"


def split_cols(x, sizes):
    idx = [int(i) for i in np.cumsum(sizes)[:-1]]
    return jnp.split(x, idx, axis=-1)


def rmsnorm(x, gain):
    xf = x.astype(jnp.float32)
    y = xf * lax.rsqrt(jnp.mean(xf * xf, axis=-1, keepdims=True) + NORM_EPS)
    return (y * gain.astype(jnp.float32)).astype(x.dtype)


def l2norm(x):
    return x * lax.rsqrt(jnp.sum(x * x, axis=-1, keepdims=True) + NORM_EPS)


def to_heads(x, n_heads):
    b, t, w = x.shape
    return x.reshape(b, t, n_heads, w // n_heads).transpose(0, 2, 1, 3)


def causal_depthwise_conv(x, w):
    k = w.shape[0]
    return lax.conv_general_dilated(
        x, w[:, None, :].astype(x.dtype), window_strides=(1,), padding=[(k - 1, 0)],
        dimension_numbers=("NWC", "WIO", "NWC"), feature_group_count=x.shape[-1])


def rotary(x, pos):
    d = x.shape[-1]
    inv_freq = jnp.power(ROPE_BASE, -jnp.linspace(0.0, 1.0, d // 2, dtype=jnp.float32))
    ang = pos[:, None] * inv_freq[None, :]
    cos, sin = jnp.cos(ang), jnp.sin(ang)
    x1, x2 = x[..., 0::2], x[..., 1::2]
    return jnp.stack([x1 * cos - x2 * sin, x1 * sin + x2 * cos], axis=-1).reshape(x.shape)


def chunked_gated_delta_rule(q, k, v, g, beta):
    b, h, t, dk = q.shape
    dv = v.shape[-1]
    n = t // CHUNK
    q = (q * dk ** -0.5).reshape(b, h, n, CHUNK, dk)
    k = k.reshape(b, h, n, CHUNK, dk)
    v = v.reshape(b, h, n, CHUNK, dv)
    beta = beta.reshape(b, h, n, CHUNK, 1)
    gc = jnp.cumsum(g.reshape(b, h, n, CHUNK), axis=-1)
    causal = jnp.tril(jnp.ones((CHUNK, CHUNK), dtype=bool))
    strict = jnp.tril(jnp.ones((CHUNK, CHUNK), dtype=bool), -1)
    decay = jnp.exp(jnp.where(causal, gc[..., :, None] - gc[..., None, :], -jnp.inf))
    kb = k * beta
    a = jnp.where(strict, jnp.einsum('bhncd,bhnsd->bhncs', kb, k) * decay, 0.0)
    eye = jnp.broadcast_to(jnp.eye(CHUNK, dtype=a.dtype), a.shape)
    tmat = lax.linalg.triangular_solve(a, eye, left_side=True, lower=True, unit_diagonal=True)
    u = jnp.einsum('bhncs,bhnse->bhnce', tmat, v * beta)
    w = jnp.einsum('bhncs,bhnsd->bhncd', tmat, kb * jnp.exp(gc)[..., None])
    scores = jnp.einsum('bhncd,bhnsd->bhncs', q, k) * decay
    q_dec = q * jnp.exp(gc)[..., None]
    k_tail = k * jnp.exp(gc[..., -1:] - gc)[..., None]
    g_last = jnp.exp(gc[..., -1])

    def step(s, xs):
        qd_n, sc_n, w_n, u_n, kt_n, gl_n = xs
        v_new = u_n - jnp.einsum('bhcd,bhde->bhce', w_n, s)
        o_n = jnp.einsum('bhcd,bhde->bhce', qd_n, s) + jnp.einsum('bhcs,bhse->bhce', sc_n, v_new)
        s = s * gl_n[..., None, None] + jnp.einsum('bhcd,bhce->bhde', kt_n, v_new)
        return s, o_n

    xs = tuple(jnp.moveaxis(z, 2, 0) for z in (q_dec, scores, w, u, k_tail, g_last))
    s0 = jnp.zeros((b, h, dk, dv), dtype=q.dtype)
    _, o = lax.scan(step, s0, xs)
    return jnp.moveaxis(o, 0, 2).reshape(b, h, t, dv)


def chunked_retention(q, k, v, log_gamma):
    b, h, t, dk = q.shape
    dv = v.shape[-1]
    n = t // CHUNK
    q = q.reshape(b, h, n, CHUNK, dk)
    k = (k * dk ** -0.5).reshape(b, h, n, CHUNK, dk)
    v = v.reshape(b, h, n, CHUNK, dv)
    pos = jnp.arange(CHUNK, dtype=jnp.float32)
    lg = log_gamma[:, None]
    causal = jnp.tril(jnp.ones((CHUNK, CHUNK), dtype=bool))
    decay = jnp.exp(jnp.where(causal, (pos[:, None] - pos[None, :]) * lg[:, :, None], -jnp.inf))
    scores = jnp.einsum('bhncd,bhnsd->bhncs', q, k) * decay[:, None]
    intra = jnp.einsum('bhncs,bhnse->bhnce', scores, v)
    q_dec = q * jnp.exp((pos + 1.0) * lg)[:, None, :, None]
    k_dec = k * jnp.exp((CHUNK - 1.0 - pos) * lg)[:, None, :, None]
    gamma_chunk = jnp.exp(CHUNK * log_gamma)[:, None, None]

    def step(s, xs):
        qd_n, kd_n, v_n, in_n = xs
        o_n = in_n + jnp.einsum('bhcd,bhde->bhce', qd_n, s)
        s = s * gamma_chunk + jnp.einsum('bhcd,bhce->bhde', kd_n, v_n)
        return s, o_n

    xs = tuple(jnp.moveaxis(z, 2, 0) for z in (q_dec, k_dec, v, intra))
    s0 = jnp.zeros((b, h, dk, dv), dtype=q.dtype)
    _, o = lax.scan(step, s0, xs)
    return jnp.moveaxis(o, 0, 2).reshape(b, h, t, dv)


def chunked_gla(q, k, v, g):
    b, h, t, dk = q.shape
    dv = v.shape[-1]
    n = t // CHUNK
    q = (q * dk ** -0.5).reshape(b, h, n, CHUNK, dk)
    k = k.reshape(b, h, n, CHUNK, dk)
    v = v.reshape(b, h, n, CHUNK, dv)
    gc = jnp.cumsum(g.reshape(b, h, n, CHUNK, dk), axis=-2)
    g_last = gc[..., -1, :]
    q_g = q * jnp.exp(gc)
    k_inv = k * jnp.exp(-gc)
    k_tail = k * jnp.exp(g_last[..., None, :] - gc)
    causal = jnp.tril(jnp.ones((CHUNK, CHUNK), dtype=bool))
    scores = jnp.where(causal, jnp.einsum('bhncd,bhnsd->bhncs', q_g, k_inv), 0.0)
    intra = jnp.einsum('bhncs,bhnse->bhnce', scores, v)

    def step(s, xs):
        qg_n, kt_n, v_n, in_n, gl_n = xs
        o_n = in_n + jnp.einsum('bhcd,bhde->bhce', qg_n, s)
        s = s * jnp.exp(gl_n)[..., None] + jnp.einsum('bhcd,bhce->bhde', kt_n, v_n)
        return s, o_n

    xs = tuple(jnp.moveaxis(z, 2, 0) for z in (q_g, k_tail, v, intra, g_last))
    s0 = jnp.zeros((b, h, dk, dv), dtype=q.dtype)
    _, o = lax.scan(step, s0, xs)
    return jnp.moveaxis(o, 0, 2).reshape(b, h, t, dv)


def even_mixer(h, w_in, conv_w, a_log, dt_bias, norm_a, norm_b, w_out):
    b, t, _ = h.shape
    proj = jnp.einsum('btd,de->bte', h, w_in).astype(jnp.float32)
    qkv_a, z_a, beta_logit, a_in, q_b, k_b, v_b, g_b = split_cols(proj, AB_SPLITS)
    qkv_a = jax.nn.silu(causal_depthwise_conv(qkv_a, conv_w.astype(jnp.float32)))
    q_a, k_a, v_a = split_cols(qkv_a, (A_KW, A_KW, A_VW))
    q_a = l2norm(to_heads(q_a, A_HEADS))
    k_a = l2norm(to_heads(k_a, A_HEADS))
    v_a = to_heads(v_a, A_HEADS)
    beta = jax.nn.sigmoid(beta_logit).transpose(0, 2, 1)
    g = (-jnp.exp(a_log.astype(jnp.float32))
         * jax.nn.softplus(a_in + dt_bias.astype(jnp.float32))).transpose(0, 2, 1)
    o_a = chunked_gated_delta_rule(q_a, k_a, v_a, g, beta).transpose(0, 2, 1, 3)
    o_a = rmsnorm(o_a, norm_a) * jax.nn.silu(z_a.reshape(b, t, A_HEADS, A_DV))
    pos = jnp.arange(t, dtype=jnp.float32)
    log_gamma = jnp.log1p(-jnp.exp2(-5.0 - jnp.arange(B_HEADS, dtype=jnp.float32)))
    q_b = rotary(to_heads(q_b, B_HEADS), pos)
    k_b = rotary(to_heads(k_b, B_HEADS), pos)
    o_b = chunked_retention(q_b, k_b, to_heads(v_b, B_HEADS), log_gamma).transpose(0, 2, 1, 3)
    o_b = rmsnorm(o_b, norm_b) * jax.nn.silu(g_b.reshape(b, t, B_HEADS, B_DV))
    mixed = jnp.concatenate([o_a.reshape(b, t, A_VW), o_b.reshape(b, t, B_VW)], axis=-1)
    return jnp.einsum('bte,ed->btd', mixed.astype(h.dtype), w_out)


def odd_mixer(h, w_in, w_gk_down, w_gk_up, b_gk, norm_c, w_out):
    b, t, _ = h.shape
    proj = jnp.einsum('btd,de->bte', h, w_in).astype(jnp.float32)
    q, k, v, r = split_cols(proj, C_SPLITS)
    gk_logit = jnp.einsum('btr,rk->btk', jnp.einsum('btd,dr->btr', h, w_gk_down), w_gk_up) + b_gk
    gk = jax.nn.log_sigmoid(gk_logit.astype(jnp.float32)) / GK_NORMALIZER
    o = chunked_gla(to_heads(q, C_HEADS), to_heads(k, C_HEADS), to_heads(v, C_HEADS),
                    to_heads(gk, C_HEADS)).transpose(0, 2, 1, 3)
    o = rmsnorm(o, norm_c) * jax.nn.silu(r.reshape(b, t, C_HEADS, C_DV))
    return jnp.einsum('bte,ed->btd', o.reshape(b, t, C_VW).astype(h.dtype), w_out)


def squared_relu_mlp(h, w_up, w_down):
    return jnp.einsum('btf,fd->btd', jnp.square(jax.nn.relu(jnp.einsum('btd,df->btf', h, w_up))), w_down)


def setup_inputs(seed: int = 0) -> dict:
    key = jax.random.key(seed)
    ks = jax.random.split(key, 19)
    ne = (DEPTH + 1) // 2
    no = DEPTH // 2

    def dense(k, shape, fan_in):
        return jax.random.normal(k, shape, jnp.float32) * fan_in ** -0.5

    def gain(k, shape):
        return 1.0 + 0.02 * jax.random.normal(k, shape, jnp.float32)

    dt = jnp.exp(jax.random.uniform(ks[9], (ne, A_HEADS), jnp.float32,
                                    minval=math.log(1e-3), maxval=math.log(1e-1)))
    return {
        "x": jax.random.normal(ks[0], (BATCH, SEQ, D_MODEL), jnp.float32),
        "norm_mix": gain(ks[1], (DEPTH, D_MODEL)),
        "norm_mlp": gain(ks[2], (DEPTH, D_MODEL)),
        "norm_final": gain(ks[3], (D_MODEL,)),
        "w_up": dense(ks[4], (DEPTH, D_MODEL, D_FF), D_MODEL),
        "w_down": dense(ks[5], (DEPTH, D_FF, D_MODEL), D_FF),
        "w_in_ab": dense(ks[6], (ne, D_MODEL, IN_AB), D_MODEL),
        "conv_a": dense(ks[7], (ne, CONV_K, A_CONV_W), CONV_K),
        "a_log": jnp.log(jax.random.uniform(ks[8], (ne, A_HEADS), jnp.float32, minval=1.0, maxval=16.0)),
        "dt_bias": dt + jnp.log(-jnp.expm1(-dt)),
        "norm_a": gain(ks[10], (ne, A_DV)),
        "norm_b": gain(ks[11], (ne, B_DV)),
        "w_out_ab": dense(ks[12], (ne, A_VW + B_VW, D_MODEL), A_VW + B_VW),
        "w_in_c": dense(ks[13], (no, D_MODEL, IN_C), D_MODEL),
        "w_gk_down": dense(ks[14], (no, D_MODEL, GK_RANK), D_MODEL),
        "w_gk_up": dense(ks[15], (no, GK_RANK, C_KW), GK_RANK),
        "b_gk": 0.01 * jax.random.normal(ks[16], (no, C_KW), jnp.float32),
        "norm_c": gain(ks[17], (no, C_DV)),
        "w_out_c": dense(ks[18], (no, C_VW, D_MODEL), C_VW),
    }


def reference(x, norm_mix, norm_mlp, norm_final, w_up, w_down, w_in_ab, conv_a, a_log,
              dt_bias, norm_a, norm_b, w_out_ab, w_in_c, w_gk_down, w_gk_up, b_gk,
              norm_c, w_out_c):
    for layer in range(DEPTH):
        i = layer // 2
        h = rmsnorm(x, norm_mix[layer])
        if layer % 2 == 0:
            mix = even_mixer(h, w_in_ab[i], conv_a[i], a_log[i], dt_bias[i],
                             norm_a[i], norm_b[i], w_out_ab[i])
        else:
            mix = odd_mixer(h, w_in_c[i], w_gk_down[i], w_gk_up[i], b_gk[i],
                            norm_c[i], w_out_c[i])
        x = x + mix.astype(x.dtype)
        x = x + squared_relu_mlp(rmsnorm(x, norm_mlp[layer]), w_up[layer], w_down[layer]).astype(x.dtype)
    return rmsnorm(x, norm_final)
```

```python
import functools
import math

import jax
import jax.numpy as jnp
from jax import lax
from jax.experimental import pallas as pl
from jax.experimental.pallas import tpu as pltpu

F32 = jnp.float32
BF16 = jnp.bfloat16

NORM_EPS = 1e-6
ROPE_BASE = 10000.0
CONV_K = 4
GK_RANK = 16
GK_NORMALIZER = 16.0
A_HEAD_DIM = 128
B_HEAD_DIM = 256
C_HEADS = 4
LANES = 128
SUBLANES = 8
A_CHUNK = 128
B_CHUNK = 256
C_CHUNK = 128
MIB = 1024 * 1024


def _params(semantics, vmem_mib=None):
    kwargs = dict(dimension_semantics=semantics)
    if vmem_mib is not None:
        kwargs["vmem_limit_bytes"] = vmem_mib * MIB
    return pltpu.CompilerParams(**kwargs)


def _dot(a, b):
    return jnp.dot(a.astype(BF16), b.astype(BF16), preferred_element_type=F32)


def _dot_nt(a, b):
    return lax.dot_general(a.astype(BF16), b.astype(BF16), (((1,), (1,)), ((), ())),
                           preferred_element_type=F32)


def _dot_tn(a, b):
    return lax.dot_general(a.astype(BF16), b.astype(BF16), (((0,), (0,)), ((), ())),
                           preferred_element_type=F32)


def _sigmoid(x):
    return 1.0 / (1.0 + jnp.exp(-x))


def _silu(x):
    return x * _sigmoid(x)


def _softplus(x):
    return jnp.maximum(x, 0.0) + jnp.log1p(jnp.exp(-jnp.abs(x)))


def _split_dot(mask_bf16, g, pieces):
    acc = None
    rem = g
    for _ in range(pieces):
        part = rem.astype(BF16)
        term = jnp.dot(mask_bf16, part, preferred_element_type=F32)
        acc = term if acc is None else acc + term
        rem = rem - part.astype(F32)
    return acc


def _chunk_cumsum_mask(n, chunk):
    ii = lax.broadcasted_iota(jnp.int32, (n, n), 0)
    jj = lax.broadcasted_iota(jnp.int32, (n, n), 1)
    same = (ii // chunk) == (jj // chunk)
    return jnp.where(same & (ii >= jj), 1.0, 0.0).astype(BF16)


def _rmsnorm_kernel(x_ref, g_ref, o_ref):
    x = x_ref[...]
    ms = jnp.mean(x * x, axis=-1, keepdims=True)
    o_ref[...] = (x * lax.rsqrt(ms + NORM_EPS) * g_ref[...]).astype(o_ref.dtype)


def _rmsnorm(x2d, gain, out_dtype, tm=256):
    m, d = x2d.shape
    tm = min(tm, m)
    return pl.pallas_call(
        _rmsnorm_kernel,
        out_shape=jax.ShapeDtypeStruct((m, d), out_dtype),
        grid=(m // tm,),
        in_specs=[pl.BlockSpec((tm, d), lambda i: (i, 0)),
                  pl.BlockSpec((1, d), lambda i: (0, 0))],
        out_specs=pl.BlockSpec((tm, d), lambda i: (i, 0)),
        compiler_params=_params(("parallel",)),
        name="rmsnorm",
    )(x2d, gain.reshape(1, d).astype(F32))


def _matmul_kernel(*refs, nk, act, has_res):
    a_ref, b_ref = refs[0], refs[1]
    res_ref = refs[2] if has_res else None
    o_ref = refs[2 + int(has_res)]

    def finish(acc):
        if act == "relu2":
            r = jnp.maximum(acc, 0.0)
            acc = r * r
        if has_res:
            acc = res_ref[...] + acc
        o_ref[...] = acc.astype(o_ref.dtype)

    if nk == 1:
        finish(jnp.dot(a_ref[...], b_ref[...], preferred_element_type=F32))
    else:
        acc_ref = refs[3 + int(has_res)]
        k = pl.program_id(2)

        @pl.when(k == 0)
        def _zero():
            acc_ref[...] = jnp.zeros_like(acc_ref)

        acc_ref[...] += jnp.dot(a_ref[...], b_ref[...], preferred_element_type=F32)

        @pl.when(k == nk - 1)
        def _store():
            finish(acc_ref[...])


def _matmul(a, b, *, res=None, act=None, out_dtype=F32, tm=1024, tn=1024, tk=None, vmem_mib=56):
    m, kdim = a.shape
    n = b.shape[1]
    tm, tn = min(tm, m), min(tn, n)
    tk = kdim if tk is None else min(tk, kdim)
    assert m % tm == 0 and n % tn == 0 and kdim % tk == 0
    nk = kdim // tk
    has_res = res is not None
    if nk == 1:
        grid = (m // tm, n // tn)
        a_spec = pl.BlockSpec((tm, tk), lambda i, j: (i, 0))
        b_spec = pl.BlockSpec((tk, tn), lambda i, j: (0, j))
        o_spec = pl.BlockSpec((tm, tn), lambda i, j: (i, j))
        scratch = []
        sem = ("parallel", "parallel")
    else:
        grid = (m // tm, n // tn, nk)
        a_spec = pl.BlockSpec((tm, tk), lambda i, j, k: (i, k))
        b_spec = pl.BlockSpec((tk, tn), lambda i, j, k: (k, j))
        o_spec = pl.BlockSpec((tm, tn), lambda i, j, k: (i, j))
        scratch = [pltpu.VMEM((tm, tn), F32)]
        sem = ("parallel", "parallel", "arbitrary")
    in_specs = [a_spec, b_spec] + ([o_spec] if has_res else [])
    args = (a, b) + ((res,) if has_res else ())
    return pl.pallas_call(
        functools.partial(_matmul_kernel, nk=nk, act=act, has_res=has_res),
        out_shape=jax.ShapeDtypeStruct((m, n), out_dtype),
        grid=grid,
        in_specs=in_specs,
        out_specs=o_spec,
        scratch_shapes=scratch,
        compiler_params=_params(sem, vmem_mib),
        name="matmul",
    )(*args)


def _rope_kernel(inv_ref, cos_ref, sin_ref, *, tt):
    pos = (lax.broadcasted_iota(jnp.int32, cos_ref.shape, 0) + pl.program_id(0) * tt).astype(F32)
    ang = pos * inv_ref[...]
    cos_ref[...] = jnp.cos(ang)
    sin_ref[...] = jnp.sin(ang)


def _rope_tables(t, half, tt=512):
    tt = min(tt, t)
    inv_freq = jnp.power(ROPE_BASE, -jnp.linspace(0.0, 1.0, half, dtype=F32)).reshape(1, half)
    return pl.pallas_call(
        functools.partial(_rope_kernel, tt=tt),
        out_shape=(jax.ShapeDtypeStruct((t, half), F32), jax.ShapeDtypeStruct((t, half), F32)),
        grid=(t // tt,),
        in_specs=[pl.BlockSpec((1, half), lambda i: (0, 0))],
        out_specs=(pl.BlockSpec((tt, half), lambda i: (i, 0)), pl.BlockSpec((tt, half), lambda i: (i, 0))),
        compiler_params=_params(("parallel",)),
        name="rope_tables",
    )(inv_freq)


def _gates_kernel(x_ref, alog_ref, dtb_ref, o_ref, *, n_heads, chunk):
    x = x_ref[...]
    beta = _sigmoid(x)
    g = -jnp.exp(alog_ref[...]) * _softplus(x + dtb_ref[...])
    gc = _split_dot(_chunk_cumsum_mask(x.shape[0], chunk), g, 3)
    lane = lax.broadcasted_iota(jnp.int32, x.shape, 1)
    o_ref[...] = jnp.where(lane < n_heads, beta, gc)


def _gates(ba, a_log, dt_bias, n_heads, chunk, tg=256):
    m, w = ba.shape
    tg = min(tg, m)
    alog_p = jnp.zeros((1, w), F32).at[0, n_heads:2 * n_heads].set(a_log.astype(F32))
    dtb_p = jnp.zeros((1, w), F32).at[0, n_heads:2 * n_heads].set(dt_bias.astype(F32))
    return pl.pallas_call(
        functools.partial(_gates_kernel, n_heads=n_heads, chunk=chunk),
        out_shape=jax.ShapeDtypeStruct((m, w), F32),
        grid=(m // tg,),
        in_specs=[pl.BlockSpec((tg, w), lambda i: (i, 0)),
                  pl.BlockSpec((1, w), lambda i: (0, 0)),
                  pl.BlockSpec((1, w), lambda i: (0, 0))],
        out_specs=pl.BlockSpec((tg, w), lambda i: (i, 0)),
        compiler_params=_params(("parallel",)),
        name="deltanet_gates",
    )(ba, alog_p, dtb_p)


def _unit_lower_inverse(a, n):
    ii = lax.broadcasted_iota(jnp.int32, (n, n), 0)
    jj = lax.broadcasted_iota(jnp.int32, (n, n), 1)
    base = 16
    eye = jnp.where(ii == jj, 1.0, 0.0)
    p = jnp.where((ii // base) == (jj // base), -a, 0.0)
    t = eye + p
    width = 2
    while width < base:
        p = _dot(p, p)
        t = t + _dot(t, p)
        width *= 2
    bs = base
    while bs < n:
        e = jnp.where(((ii // (2 * bs)) == (jj // (2 * bs))) & ((ii // bs) != (jj // bs)), a, 0.0)
        t = t - _dot(_dot(t, e), t)
        bs *= 2
    return t


def _mixer_a_kernel(q_ref, k_ref, v_ref, z_ref, gate_ref, gct_ref, cwq_ref, cwk_ref, cwv_ref, na_ref,
                    o_ref, s_ref, cq_ref, ck_ref, cv_ref, *, tb, chunk, n_heads):
    h = pl.program_id(1)

    @pl.when(pl.program_id(2) == 0)
    def _init():
        s_ref[...] = jnp.zeros_like(s_ref)
        cq_ref[...] = jnp.zeros_like(cq_ref)
        ck_ref[...] = jnp.zeros_like(ck_ref)
        cv_ref[...] = jnp.zeros_like(cv_ref)

    def conv_silu(x_ref, w_ref, carry_ref):
        x = x_ref[0]
        ext = jnp.concatenate([carry_ref[...], x], axis=0)
        w = w_ref[...]
        y = w[CONV_K - 1:CONV_K] * x
        for j in range(CONV_K - 1):
            off = SUBLANES - (CONV_K - 1) + j
            y = y + w[j:j + 1] * ext[off:off + tb]
        carry_ref[...] = x[tb - SUBLANES:tb]
        return _silu(y)

    def l2norm(x):
        return x * lax.rsqrt(jnp.sum(x * x, axis=-1, keepdims=True) + NORM_EPS)

    dk = q_ref.shape[-1]
    q = l2norm(conv_silu(q_ref, cwq_ref, cq_ref)) * (dk ** -0.5)
    k = l2norm(conv_silu(k_ref, cwk_ref, ck_ref))
    v = conv_silu(v_ref, cwv_ref, cv_ref)
    z = z_ref[0]

    gt = gate_ref[0]
    lane = lax.broadcasted_iota(jnp.int32, gt.shape, 1)
    beta = jnp.sum(jnp.where(lane == h, gt, 0.0), axis=1, keepdims=True)
    gcol = jnp.sum(jnp.where(lane == h + n_heads, gt, 0.0), axis=1, keepdims=True)
    grow = gct_ref[0, pl.ds(h, 1), :]

    ii = lax.broadcasted_iota(jnp.int32, (chunk, chunk), 0)
    jj = lax.broadcasted_iota(jnp.int32, (chunk, chunk), 1)
    causal = ii >= jj
    strict = ii > jj
    gain = na_ref[...]

    s = s_ref[...]
    for c in range(tb // chunk):
        r = slice(c * chunk, (c + 1) * chunk)
        qc, kc, vc, bc, gc = q[r], k[r], v[r], beta[r], gcol[r]
        decay = jnp.where(causal, jnp.exp(jnp.where(causal, gc - grow[:, r], 0.0)), 0.0)
        kb = kc * bc
        a = jnp.where(strict, _dot_nt(kb, kc) * decay, 0.0)
        t = _unit_lower_inverse(a, chunk)
        eg = jnp.exp(gc)
        u = _dot(t, vc * bc)
        w = _dot(t, kb * eg)
        scores = _dot_nt(qc, kc) * decay
        g_last = gc[chunk - 1:chunk]
        k_tail = kc * jnp.exp(g_last - gc)
        v_new = u - _dot(w, s)
        o = _dot(qc * eg, s) + _dot(scores, v_new)
        s = s * jnp.exp(g_last) + _dot_tn(k_tail, v_new)
        on = o * lax.rsqrt(jnp.mean(o * o, axis=-1, keepdims=True) + NORM_EPS) * gain
        o_ref[0, r, :] = (on * _silu(z[r])).astype(o_ref.dtype)
    s_ref[...] = s


def _mixer_a(proj, gates, gates_t, conv_w, norm_a, n_heads, tb=256):
    b, t, _ = proj.shape
    d = A_HEAD_DIM
    tb = min(tb, t)
    chunk = min(A_CHUNK, tb)
    hh = n_heads

    def col(off):
        return pl.BlockSpec((1, tb, d), lambda bi, hi, ti, off=off: (bi, ti, off + hi))

    def cw(off):
        return pl.BlockSpec((CONV_K, d), lambda bi, hi, ti, off=off: (0, off + hi))

    return pl.pallas_call(
        functools.partial(_mixer_a_kernel, tb=tb, chunk=chunk, n_heads=hh),
        out_shape=jax.ShapeDtypeStruct((b, t, hh * d), BF16),
        grid=(b, hh, t // tb),
        in_specs=[col(0), col(hh), col(2 * hh), col(3 * hh),
                  pl.BlockSpec((1, tb, gates.shape[-1]), lambda bi, hi, ti: (bi, ti, 0)),
                  pl.BlockSpec((1, hh, tb), lambda bi, hi, ti: (bi, 0, ti)),
                  cw(0), cw(hh), cw(2 * hh),
                  pl.BlockSpec((1, d), lambda bi, hi, ti: (0, 0))],
        out_specs=pl.BlockSpec((1, tb, d), lambda bi, hi, ti: (bi, ti, hi)),
        scratch_shapes=[pltpu.VMEM((d, d), F32),
                        pltpu.VMEM((SUBLANES, d), F32),
                        pltpu.VMEM((SUBLANES, d), F32),
                        pltpu.VMEM((SUBLANES, d), F32)],
        compiler_params=_params(("parallel", "parallel", "arbitrary")),
        name="mixer_deltanet",
    )(proj, proj, proj, proj, gates, gates_t, conv_w, conv_w, conv_w, norm_a.reshape(1, d).astype(F32))


def _mixer_b_kernel(q_ref, k_ref, v_ref, g_ref, cos_ref, sin_ref, nb_ref, o_ref, s_ref, *, chunk):
    h = pl.program_id(1)

    @pl.when(pl.program_id(2) == 0)
    def _init():
        s_ref[...] = jnp.zeros_like(s_ref)

    dk = q_ref.shape[-1]
    half = dk // 2
    cos, sin = cos_ref[...], sin_ref[...]

    def rotate(x):
        x1, x2 = x[:, :half], x[:, half:]
        return jnp.concatenate([x1 * cos - x2 * sin, x1 * sin + x2 * cos], axis=-1)

    q = rotate(q_ref[0])
    k = rotate(k_ref[0] * (dk ** -0.5))
    v = v_ref[0]

    hf = jnp.full((1, 1), h, jnp.int32).astype(F32)
    lg = jnp.log1p(-jnp.exp2(-5.0 - hf))
    ii = lax.broadcasted_iota(jnp.int32, (chunk, chunk), 0)
    jj = lax.broadcasted_iota(jnp.int32, (chunk, chunk), 1)
    causal = ii >= jj
    decay = jnp.where(causal, jnp.exp(jnp.where(causal, (ii - jj).astype(F32) * lg, 0.0)), 0.0)
    pos = lax.broadcasted_iota(jnp.int32, (chunk, 1), 0).astype(F32)

    s = s_ref[...]
    scores = _dot_nt(q, k) * decay
    o = _dot(scores, v) + _dot(q * jnp.exp((pos + 1.0) * lg), s)
    s_ref[...] = s * jnp.exp(chunk * lg) + _dot_tn(k * jnp.exp((chunk - 1.0 - pos) * lg), v)
    on = o * lax.rsqrt(jnp.mean(o * o, axis=-1, keepdims=True) + NORM_EPS) * nb_ref[...]
    o_ref[0] = (on * _silu(g_ref[0])).astype(o_ref.dtype)


def _mixer_b(proj, cos, sin, norm_b, n_heads, col0):
    b, t, _ = proj.shape
    d = B_HEAD_DIM
    chunk = min(B_CHUNK, t)
    hh = n_heads
    base = col0 // d

    def col(off):
        return pl.BlockSpec((1, chunk, d), lambda bi, hi, ti, off=off: (bi, ti, base + off + hi))

    tab = pl.BlockSpec((chunk, d // 2), lambda bi, hi, ti: (ti, 0))
    return pl.pallas_call(
        functools.partial(_mixer_b_kernel, chunk=chunk),
        out_shape=jax.ShapeDtypeStruct((b, t, hh * d), BF16),
        grid=(b, hh, t // chunk),
        in_specs=[col(0), col(hh), col(2 * hh), col(3 * hh), tab, tab,
                  pl.BlockSpec((1, d), lambda bi, hi, ti: (0, 0))],
        out_specs=pl.BlockSpec((1, chunk, d), lambda bi, hi, ti: (bi, ti, hi)),
        scratch_shapes=[pltpu.VMEM((d, d), F32)],
        compiler_params=_params(("parallel", "parallel", "arbitrary")),
        name="mixer_retention",
    )(proj, proj, proj, proj, cos, sin, norm_b.reshape(1, d).astype(F32))


def _mixer_c_kernel(q_ref, k_ref, v_ref, r_ref, lr_ref, wup_ref, bgk_ref, nc_ref, o_ref, st_ref, *, chunk):
    @pl.when(pl.program_id(2) == 0)
    def _init():
        st_ref[...] = jnp.zeros_like(st_ref)

    dk = q_ref.shape[-1]
    q = q_ref[0] * (dk ** -0.5)
    k = k_ref[0]
    v = v_ref[0]

    logit = _dot(lr_ref[0], wup_ref[...]) + bgk_ref[...]
    gk = -_softplus(-logit) / GK_NORMALIZER
    gc = _split_dot(_chunk_cumsum_mask(chunk, chunk), gk, 2)
    mid = chunk // 2 - 1
    g_mid = gc[mid:mid + 1]
    g_last = gc[chunk - 1:chunk]

    ii = lax.broadcasted_iota(jnp.int32, (chunk, chunk), 0)
    jj = lax.broadcasted_iota(jnp.int32, (chunk, chunk), 1)
    scores = jnp.where(ii >= jj, _dot_nt(q * jnp.exp(gc - g_mid), k * jnp.exp(g_mid - gc)), 0.0)
    st = st_ref[...]
    o = _dot(scores, v) + _dot_nt(q * jnp.exp(gc), st)
    st_ref[...] = st * jnp.exp(g_last) + _dot_tn(v, k * jnp.exp(g_last - gc))
    on = o * lax.rsqrt(jnp.mean(o * o, axis=-1, keepdims=True) + NORM_EPS) * nc_ref[...]
    o_ref[0] = (on * _silu(r_ref[0])).astype(o_ref.dtype)


def _mixer_c(proj, lr, w_up, b_gk, norm_c, kw, vw):
    b, t, _ = proj.shape
    hh = C_HEADS
    dk, dv = kw // hh, vw // hh
    chunk = min(C_CHUNK, t)
    rw = lr.shape[-1]
    vbase = 2 * kw // dv
    return pl.pallas_call(
        functools.partial(_mixer_c_kernel, chunk=chunk),
        out_shape=jax.ShapeDtypeStruct((b, t, vw), BF16),
        grid=(b, hh, t // chunk),
        in_specs=[pl.BlockSpec((1, chunk, dk), lambda bi, hi, ti: (bi, ti, hi)),
                  pl.BlockSpec((1, chunk, dk), lambda bi, hi, ti: (bi, ti, hh + hi)),
                  pl.BlockSpec((1, chunk, dv), lambda bi, hi, ti: (bi, ti, vbase + hi)),
                  pl.BlockSpec((1, chunk, dv), lambda bi, hi, ti: (bi, ti, vbase + hh + hi)),
                  pl.BlockSpec((1, chunk, rw), lambda bi, hi, ti: (bi, ti, 0)),
                  pl.BlockSpec((rw, dk), lambda bi, hi, ti: (0, hi)),
                  pl.BlockSpec((1, dk), lambda bi, hi, ti: (0, hi)),
                  pl.BlockSpec((1, dv), lambda bi, hi, ti: (0, 0))],
        out_specs=pl.BlockSpec((1, chunk, dv), lambda bi, hi, ti: (bi, ti, hi)),
        scratch_shapes=[pltpu.VMEM((dv, dk), F32)],
        compiler_params=_params(("parallel", "parallel", "arbitrary")),
        name="mixer_gla",
    )(proj, proj, proj, proj, lr, w_up, b_gk.reshape(1, kw).astype(F32), norm_c.reshape(1, dv).astype(F32))


def _pad_cols(w, width):
    return jnp.pad(w, ((0, 0), (0, width - w.shape[1])))


def _deinterleave_heads(w, n_heads, head_dim):
    d = w.shape[0]
    return w.reshape(d, n_heads, head_dim // 2, 2).transpose(0, 1, 3, 2).reshape(d, n_heads * head_dim)


def _mlp(x2d, gain, w_up, w_down):
    hn = _rmsnorm(x2d, gain, BF16)
    hid = _matmul(hn, w_up.astype(BF16), act="relu2", out_dtype=BF16)
    return _matmul(hid, w_down.astype(BF16), res=x2d, tk=2048)


def _even_layer(x2d, b, t, gain, w_in, conv_w, a_log, dt_bias, norm_a, norm_b, w_out):
    d = x2d.shape[1]
    ha, hb = d // 256, d // 512
    akw = ha * A_HEAD_DIM
    bkw = hb * B_HEAD_DIM
    small0 = 4 * akw
    b0 = small0 + 2 * ha
    w_main = jnp.concatenate(
        [w_in[:, :small0],
         _deinterleave_heads(w_in[:, b0:b0 + bkw], hb, B_HEAD_DIM),
         _deinterleave_heads(w_in[:, b0 + bkw:b0 + 2 * bkw], hb, B_HEAD_DIM),
         w_in[:, b0 + 2 * bkw:]], axis=1).astype(BF16)
    w_small = _pad_cols(w_in[:, small0:b0], LANES).astype(BF16)

    hn = _rmsnorm(x2d, gain, BF16)
    proj = _matmul(hn, w_main).reshape(b, t, -1)
    ba = _matmul(hn, w_small)
    gates = _gates(ba, a_log, dt_bias, ha, A_CHUNK).reshape(b, t, LANES)
    gates_t = jnp.swapaxes(gates[:, :, ha:2 * ha], 1, 2)
    o_a = _mixer_a(proj, gates, gates_t, conv_w.astype(F32), norm_a, ha)
    cos, sin = _rope_tables(t, B_HEAD_DIM // 2)
    o_b = _mixer_b(proj, cos, sin, norm_b, hb, small0)
    mixed = jnp.concatenate([o_a, o_b], axis=-1).reshape(b * t, -1)
    return _matmul(mixed, w_out.astype(BF16), res=x2d, tn=512)


def _odd_layer(x2d, b, t, gain, w_in, w_gk_down, w_gk_up, b_gk, norm_c, w_out):
    d = x2d.shape[1]
    kw, vw = d // 2, d
    hn = _rmsnorm(x2d, gain, BF16)
    proj = _matmul(hn, w_in.astype(BF16)).reshape(b, t, -1)
    lr = _matmul(hn, _pad_cols(w_gk_down, LANES).astype(BF16)).reshape(b, t, LANES)
    w_up = jnp.pad(w_gk_up, ((0, LANES - w_gk_up.shape[0]), (0, 0))).astype(BF16)
    o_c = _mixer_c(proj, lr, w_up, b_gk, norm_c, kw, vw).reshape(b * t, vw)
    return _matmul(o_c, w_out.astype(BF16), res=x2d, tn=512)


def kernel(x, norm_mix, norm_mlp, norm_final, w_up, w_down, w_in_ab, conv_a, a_log, dt_bias, norm_a, norm_b,
           w_out_ab, w_in_c, w_gk_down, w_gk_up, b_gk, norm_c, w_out_c):
    b, t, d = x.shape
    depth = norm_mix.shape[0]
    x2d = x.reshape(b * t, d)
    for layer in range(depth):
        i = layer // 2
        if layer % 2 == 0:
            x2d = _even_layer(x2d, b, t, norm_mix[layer], w_in_ab[i], conv_a[i], a_log[i], dt_bias[i],
                              norm_a[i], norm_b[i], w_out_ab[i])
        else:
            x2d = _odd_layer(x2d, b, t, norm_mix[layer], w_in_c[i], w_gk_down[i], w_gk_up[i], b_gk[i],
                             norm_c[i], w_out_c[i])
        x2d = _mlp(x2d, norm_mlp[layer], w_up[layer], w_down[layer])
    return _rmsnorm(x2d, norm_final, F32).reshape(b, t, d)
```

```python
import functools
import math

import jax
import jax.numpy as jnp
from jax import lax
from jax.experimental import pallas as pl
from jax.experimental.pallas import tpu as pltpu

F32 = jnp.float32
BF16 = jnp.bfloat16

NORM_EPS = 1e-6
ROPE_BASE = 10000.0
CONV_K = 4
GK_RANK = 16
GK_NORMALIZER = 16.0
A_HEAD_DIM = 128
B_HEAD_DIM = 256
C_HEADS = 4
LANES = 128
SUBLANES = 8
MXU_COLS = 256
CONV_PROJ_ROWS = 512
A_CHUNK = 128
B_CHUNK = 256
C_CHUNK = 128
MIB = 1024 * 1024


def _params(semantics, vmem_mib=None):
    kwargs = dict(dimension_semantics=semantics)
    if vmem_mib is not None:
        kwargs["vmem_limit_bytes"] = vmem_mib * MIB
    return pltpu.CompilerParams(**kwargs)


def _dot(a, b):
    return jnp.dot(a.astype(BF16), b.astype(BF16), preferred_element_type=F32)


def _dot_nt(a, b):
    return lax.dot_general(a.astype(BF16), b.astype(BF16), (((1,), (1,)), ((), ())),
                           preferred_element_type=F32)


def _dot_tn(a, b):
    return lax.dot_general(a.astype(BF16), b.astype(BF16), (((0,), (0,)), ((), ())),
                           preferred_element_type=F32)


def _sigmoid(x):
    return 1.0 / (1.0 + jnp.exp(-x))


def _silu(x):
    return x * _sigmoid(x)


def _softplus(x):
    return jnp.maximum(x, 0.0) + jnp.log1p(jnp.exp(-jnp.abs(x)))


def _split_dot(mask_bf16, g, pieces):
    acc = None
    rem = g
    for _ in range(pieces):
        part = rem.astype(BF16)
        term = jnp.dot(mask_bf16, part, preferred_element_type=F32)
        acc = term if acc is None else acc + term
        rem = rem - part.astype(F32)
    return acc


def _chunk_cumsum_mask(n, chunk):
    ii = lax.broadcasted_iota(jnp.int32, (n, n), 0)
    jj = lax.broadcasted_iota(jnp.int32, (n, n), 1)
    same = (ii // chunk) == (jj // chunk)
    return jnp.where(same & (ii >= jj), 1.0, 0.0).astype(BF16)


def _rmsnorm_kernel(x_ref, g_ref, o_ref):
    x = x_ref[...]
    ms = jnp.mean(x * x, axis=-1, keepdims=True)
    o_ref[...] = (x * lax.rsqrt(ms + NORM_EPS) * g_ref[...]).astype(o_ref.dtype)


def _rmsnorm(x2d, gain, out_dtype, tm=256):
    m, d = x2d.shape
    tm = min(tm, m)
    return pl.pallas_call(
        _rmsnorm_kernel,
        out_shape=jax.ShapeDtypeStruct((m, d), out_dtype),
        grid=(m // tm,),
        in_specs=[pl.BlockSpec((tm, d), lambda i: (i, 0)),
                  pl.BlockSpec((1, d), lambda i: (0, 0))],
        out_specs=pl.BlockSpec((tm, d), lambda i: (i, 0)),
        compiler_params=_params(("parallel",)),
        name="rmsnorm",
    )(x2d, gain.reshape(1, d).astype(F32))


def _matmul_kernel(*refs, nk, act, has_res):
    a_ref, b_ref = refs[0], refs[1]
    res_ref = refs[2] if has_res else None
    o_ref = refs[2 + int(has_res)]

    def finish(acc):
        if act == "relu2":
            r = jnp.maximum(acc, 0.0)
            acc = r * r
        elif act == "silu":
            acc = _silu(acc)
        elif isinstance(act, float):
            acc = acc * act
        if has_res:
            acc = res_ref[...] + acc
        o_ref[...] = acc.astype(o_ref.dtype)

    if nk == 1:
        finish(jnp.dot(a_ref[...], b_ref[...], preferred_element_type=F32))
    else:
        acc_ref = refs[3 + int(has_res)]
        k = pl.program_id(2)

        @pl.when(k == 0)
        def _zero():
            acc_ref[...] = jnp.zeros_like(acc_ref)

        acc_ref[...] += jnp.dot(a_ref[...], b_ref[...], preferred_element_type=F32)

        @pl.when(k == nk - 1)
        def _store():
            finish(acc_ref[...])


def _matmul(a, b, *, res=None, act=None, out_dtype=F32, tm=1024, tn=1024, tk=None, vmem_mib=56):
    m, kdim = a.shape
    n = b.shape[1]
    tm, tn = min(tm, m), min(tn, n)
    tk = kdim if tk is None else min(tk, kdim)
    assert m % tm == 0 and n % tn == 0 and kdim % tk == 0
    nk = kdim // tk
    has_res = res is not None
    if nk == 1:
        grid = (m // tm, n // tn)
        a_spec = pl.BlockSpec((tm, tk), lambda i, j: (i, 0))
        b_spec = pl.BlockSpec((tk, tn), lambda i, j: (0, j))
        o_spec = pl.BlockSpec((tm, tn), lambda i, j: (i, j))
        scratch = []
        sem = ("parallel", "parallel")
    else:
        grid = (m // tm, n // tn, nk)
        a_spec = pl.BlockSpec((tm, tk), lambda i, j, k: (i, k))
        b_spec = pl.BlockSpec((tk, tn), lambda i, j, k: (k, j))
        o_spec = pl.BlockSpec((tm, tn), lambda i, j, k: (i, j))
        scratch = [pltpu.VMEM((tm, tn), F32)]
        sem = ("parallel", "parallel", "arbitrary")
    in_specs = [a_spec, b_spec] + ([o_spec] if has_res else [])
    args = (a, b) + ((res,) if has_res else ())
    return pl.pallas_call(
        functools.partial(_matmul_kernel, nk=nk, act=act, has_res=has_res),
        out_shape=jax.ShapeDtypeStruct((m, n), out_dtype),
        grid=grid,
        in_specs=in_specs,
        out_specs=o_spec,
        scratch_shapes=scratch,
        compiler_params=_params(sem, vmem_mib),
        name="matmul",
    )(*args)


def _conv_proj_kernel(a_ref, b_ref, cw_ref, o_ref, carry_ref, raw_ref, *, tiles_per_seq, norm_dim, scale):
    i, j = pl.program_id(0), pl.program_id(1)
    tm, tn = o_ref.shape
    rows_per_dot = min(CONV_PROJ_ROWS, tm)
    first_tile = lax.rem(i, tiles_per_seq) == 0

    def epilogue(acc, r0, c0):
        cols = slice(c0, c0 + MXU_COLS)
        if r0 == 0:
            prev = carry_ref[j, :, cols]
            prev = jnp.where(first_tile, jnp.zeros_like(prev), prev)
        else:
            prev = raw_ref[r0 - SUBLANES:r0, cols]
        if r0 + rows_per_dot == tm:
            carry_ref[j, :, cols] = acc[rows_per_dot - SUBLANES:]
        ext = jnp.concatenate([prev, acc], axis=0)
        w = cw_ref[:, cols]
        y = w[CONV_K - 1:CONV_K] * acc
        for tap in range(CONV_K - 1):
            off = SUBLANES - (CONV_K - 1) + tap
            y = y + w[tap:tap + 1] * ext[off:off + rows_per_dot]
        y = _silu(y)
        if norm_dim is not None:
            segs = []
            for s0 in range(0, MXU_COLS, norm_dim):
                seg = y[:, s0:s0 + norm_dim]
                segs.append(seg * (lax.rsqrt(jnp.sum(seg * seg, axis=-1, keepdims=True) + NORM_EPS) * scale))
            y = jnp.concatenate(segs, axis=1)
        o_ref[r0:r0 + rows_per_dot, cols] = y.astype(o_ref.dtype)

    subtiles = [(r0, c0) for r0 in range(0, tm, rows_per_dot) for c0 in range(0, tn, MXU_COLS)]
    pending = None
    for r0, c0 in subtiles:
        raw_ref[r0:r0 + rows_per_dot, c0:c0 + MXU_COLS] = jnp.dot(
            a_ref[r0:r0 + rows_per_dot, :], b_ref[:, c0:c0 + MXU_COLS], preferred_element_type=F32)
        if pending is not None:
            pr, pc = pending
            epilogue(raw_ref[pr:pr + rows_per_dot, pc:pc + MXU_COLS], pr, pc)
        pending = (r0, c0)
    pr, pc = pending
    epilogue(raw_ref[pr:pr + rows_per_dot, pc:pc + MXU_COLS], pr, pc)


def _conv_proj(a, b, conv_w, seq_len, *, norm_dim=None, scale=1.0, tm=1024, tn=1024, vmem_mib=56):
    m, kdim = a.shape
    n = b.shape[1]
    tm, tn = min(tm, seq_len), min(tn, n)
    assert m % tm == 0 and n % tn == 0 and seq_len % tm == 0
    return pl.pallas_call(
        functools.partial(_conv_proj_kernel, tiles_per_seq=seq_len // tm, norm_dim=norm_dim, scale=scale),
        out_shape=jax.ShapeDtypeStruct((m, n), F32),
        grid=(m // tm, n // tn),
        in_specs=[pl.BlockSpec((tm, kdim), lambda i, j: (i, 0)),
                  pl.BlockSpec((kdim, tn), lambda i, j: (0, j)),
                  pl.BlockSpec((CONV_K, tn), lambda i, j: (0, j))],
        out_specs=pl.BlockSpec((tm, tn), lambda i, j: (i, j)),
        scratch_shapes=[pltpu.VMEM((n // tn, SUBLANES, tn), F32), pltpu.VMEM((tm, tn), F32)],
        compiler_params=_params(("arbitrary", "arbitrary"), vmem_mib),
        name="conv_proj",
    )(a, b, conv_w)


def _rope_proj_kernel(a_ref, b_ref, cos_ref, sin_ref, o_ref, *, head_dim, scale):
    acc = jnp.dot(a_ref[...], b_ref[...], preferred_element_type=F32)
    cos, sin = cos_ref[...], sin_ref[...]
    half = head_dim // 2
    parts = []
    for c0 in range(0, acc.shape[1], head_dim):
        x1, x2 = acc[:, c0:c0 + half], acc[:, c0 + half:c0 + head_dim]
        parts += [(x1 * cos - x2 * sin) * scale, (x1 * sin + x2 * cos) * scale]
    o_ref[...] = jnp.concatenate(parts, axis=1).astype(o_ref.dtype)


def _rope_proj(a, b, cos, sin, seq_len, head_dim, *, scale=1.0, tm=1024, tn=1024, vmem_mib=56):
    m, kdim = a.shape
    n = b.shape[1]
    tm, tn = min(tm, seq_len), min(tn, n)
    assert m % tm == 0 and n % tn == 0 and seq_len % tm == 0 and tn % head_dim == 0
    tiles_per_seq = seq_len // tm
    tab = pl.BlockSpec((tm, head_dim // 2), lambda i, j: (i % tiles_per_seq, 0))
    return pl.pallas_call(
        functools.partial(_rope_proj_kernel, head_dim=head_dim, scale=scale),
        out_shape=jax.ShapeDtypeStruct((m, n), F32),
        grid=(m // tm, n // tn),
        in_specs=[pl.BlockSpec((tm, kdim), lambda i, j: (i, 0)),
                  pl.BlockSpec((kdim, tn), lambda i, j: (0, j)),
                  tab, tab],
        out_specs=pl.BlockSpec((tm, tn), lambda i, j: (i, j)),
        compiler_params=_params(("parallel", "parallel"), vmem_mib),
        name="rope_proj",
    )(a, b, cos, sin)


def _rope_kernel(inv_ref, cos_ref, sin_ref, *, tt):
    pos = (lax.broadcasted_iota(jnp.int32, cos_ref.shape, 0) + pl.program_id(0) * tt).astype(F32)
    ang = pos * inv_ref[...]
    cos_ref[...] = jnp.cos(ang)
    sin_ref[...] = jnp.sin(ang)


def _rope_tables(t, half, tt=512):
    tt = min(tt, t)
    inv_freq = jnp.power(ROPE_BASE, -jnp.linspace(0.0, 1.0, half, dtype=F32)).reshape(1, half)
    return pl.pallas_call(
        functools.partial(_rope_kernel, tt=tt),
        out_shape=(jax.ShapeDtypeStruct((t, half), F32), jax.ShapeDtypeStruct((t, half), F32)),
        grid=(t // tt,),
        in_specs=[pl.BlockSpec((1, half), lambda i: (0, 0))],
        out_specs=(pl.BlockSpec((tt, half), lambda i: (i, 0)), pl.BlockSpec((tt, half), lambda i: (i, 0))),
        compiler_params=_params(("parallel",)),
        name="rope_tables",
    )(inv_freq)


def _gates_kernel(x_ref, alog_ref, dtb_ref, o_ref, *, n_heads, chunk):
    x = x_ref[...]
    beta = _sigmoid(x)
    g = -jnp.exp(alog_ref[...]) * _softplus(x + dtb_ref[...])
    gc = _split_dot(_chunk_cumsum_mask(x.shape[0], chunk), g, 3)
    lane = lax.broadcasted_iota(jnp.int32, x.shape, 1)
    o_ref[...] = jnp.where(lane < n_heads, beta, gc)


def _gates(ba, a_log, dt_bias, n_heads, chunk, tg=256):
    m, w = ba.shape
    tg = min(tg, m)
    alog_p = jnp.zeros((1, w), F32).at[0, n_heads:2 * n_heads].set(a_log.astype(F32))
    dtb_p = jnp.zeros((1, w), F32).at[0, n_heads:2 * n_heads].set(dt_bias.astype(F32))
    return pl.pallas_call(
        functools.partial(_gates_kernel, n_heads=n_heads, chunk=chunk),
        out_shape=jax.ShapeDtypeStruct((m, w), F32),
        grid=(m // tg,),
        in_specs=[pl.BlockSpec((tg, w), lambda i: (i, 0)),
                  pl.BlockSpec((1, w), lambda i: (0, 0)),
                  pl.BlockSpec((1, w), lambda i: (0, 0))],
        out_specs=pl.BlockSpec((tg, w), lambda i: (i, 0)),
        compiler_params=_params(("parallel",)),
        name="deltanet_gates",
    )(ba, alog_p, dtb_p)


def _unit_lower_inverses(mats, n):
    ii = lax.broadcasted_iota(jnp.int32, (n, n), 0)
    jj = lax.broadcasted_iota(jnp.int32, (n, n), 1)
    base = 16
    eye = jnp.where(ii == jj, 1.0, 0.0)
    diag_blocks = (ii // base) == (jj // base)
    ps = [jnp.where(diag_blocks, -a, 0.0) for a in mats]
    ts = [eye + p for p in ps]
    width = 2
    while width < base:
        ps = [_dot(p, p) for p in ps]
        ts = [t + _dot(t, p) for t, p in zip(ts, ps)]
        width *= 2
    bs = base
    while bs < n:
        off_blocks = ((ii // (2 * bs)) == (jj // (2 * bs))) & ((ii // bs) != (jj // bs))
        xs = [_dot(t, jnp.where(off_blocks, a, 0.0)) for t, a in zip(ts, mats)]
        ts = [t - _dot(x, t) for t, x in zip(ts, xs)]
        bs *= 2
    return ts


def _mixer_a_kernel(q_ref, k_ref, v_ref, z_ref, gate_ref, gct_ref, na_ref, o_ref, s_ref, *, tb, n_heads):
    d = A_HEAD_DIM
    heads = range(n_heads)

    @pl.when(pl.program_id(1) == 0)
    def _init():
        s_ref[...] = jnp.zeros_like(s_ref)

    q_all, k_all, v_all, z_all = q_ref[0], k_ref[0], v_ref[0], z_ref[0]
    gt = gate_ref[0]
    lane = lax.broadcasted_iota(jnp.int32, gt.shape, 1)

    ii = lax.broadcasted_iota(jnp.int32, (tb, tb), 0)
    jj = lax.broadcasted_iota(jnp.int32, (tb, tb), 1)
    causal = ii >= jj
    strict = ii > jj
    gain = na_ref[...]

    def head_cols(x, h):
        return x[:, h * d:(h + 1) * d]

    q = [head_cols(q_all, h) for h in heads]
    k = [head_cols(k_all, h) for h in heads]
    beta =[jnp.sum(jnp.where(lane == h, gt, 0.0), axis=1, keepdims=True) for h in heads]
    gc = [jnp.sum(jnp.where(lane == h + n_heads, gt, 0.0), axis=1, keepdims=True) for h in heads]
    grow = gct_ref[0]
    decay = [jnp.where(causal, jnp.exp(jnp.where(causal, gc[h] - grow[h:h + 1], 0.0)), 0.0) for h in heads]
    kb = [k[h] * beta[h] for h in heads]
    eg = [jnp.exp(gc[h]) for h in heads]

    kq = [_dot_nt(jnp.concatenate([kb[h], q[h]], axis=0), k[h]) for h in heads]
    a = [jnp.where(strict, kq[h][:tb] * decay[h], 0.0) for h in heads]
    scores = [kq[h][tb:] * decay[h] for h in heads]
    t = _unit_lower_inverses(a, tb)
    uw = [_dot(t[h], jnp.concatenate([head_cols(v_all, h) * beta[h], kb[h] * eg[h]], axis=1)) for h in heads]
    s = [s_ref[h] for h in heads]
    ws = [_dot(jnp.concatenate([uw[h][:, d:], q[h] * eg[h]], axis=0), s[h]) for h in heads]
    v_new = [uw[h][:, :d] - ws[h][:tb] for h in heads]
    o = [ws[h][tb:] + _dot(scores[h], v_new[h]) for h in heads]
    g_last = [gc[h][tb - 1:tb] for h in heads]
    kv = [_dot_tn(k[h] * jnp.exp(g_last[h] - gc[h]), v_new[h]) for h in heads]
    for h in heads:
        s_ref[h] = s[h] * jnp.exp(g_last[h]) + kv[h]
        on = o[h] * lax.rsqrt(jnp.mean(o[h] * o[h], axis=-1, keepdims=True) + NORM_EPS) * gain
        o_ref[0, :, h * d:(h + 1) * d] = (on * head_cols(z_all, h)).astype(o_ref.dtype)


def _mixer_a(q, k, v, z, z_block, gates, gates_t, norm_a, n_heads):
    b, t, hd = q.shape
    d = A_HEAD_DIM
    tb = min(A_CHUNK, t)
    hh = n_heads
    col = pl.BlockSpec((1, tb, hd), lambda bi, ti: (bi, ti, 0))
    return pl.pallas_call(
        functools.partial(_mixer_a_kernel, tb=tb, n_heads=hh),
        out_shape=jax.ShapeDtypeStruct((b, t, hd), BF16),
        grid=(b, t // tb),
        in_specs=[col, col, col, pl.BlockSpec((1, tb, hd), lambda bi, ti: (bi, ti, z_block)),
                  pl.BlockSpec((1, tb, gates.shape[-1]), lambda bi, ti: (bi, ti, 0)),
                  pl.BlockSpec((1, hh, tb), lambda bi, ti: (bi, 0, ti)),
                  pl.BlockSpec((1, d), lambda bi, ti: (0, 0))],
        out_specs=col,
        scratch_shapes=[pltpu.VMEM((hh, d, d), F32)],
        compiler_params=_params(("parallel", "arbitrary")),
        name="mixer_deltanet",
    )(q, k, v, z, gates, gates_t, norm_a.reshape(1, d).astype(F32))


def _mixer_b_kernel(q_ref, k_ref, v_ref, g_ref, nb_ref, o_ref, s_ref, decay_ref, *, chunk, n_heads):
    d = B_HEAD_DIM
    heads = range(n_heads)
    log_gamma = [math.log1p(-(2.0 ** (-5.0 - h))) for h in heads]

    @pl.when(pl.program_id(1) == 0)
    def _init():
        s_ref[...] = jnp.zeros_like(s_ref)
        ii = lax.broadcasted_iota(jnp.int32, (chunk, chunk), 0)
        jj = lax.broadcasted_iota(jnp.int32, (chunk, chunk), 1)
        causal = ii >= jj
        dist = jnp.where(causal, ii - jj, 0).astype(F32)
        for h in heads:
            decay_ref[h] = jnp.where(causal, jnp.exp(dist * log_gamma[h]), 0.0)

    pos = lax.broadcasted_iota(jnp.int32, (chunk, 1), 0).astype(F32)
    q_all, k_all, v_all, g_all = q_ref[0], k_ref[0], v_ref[0], g_ref[0]

    def head_cols(x, h):
        return x[:, h * d:(h + 1) * d]

    q = [head_cols(q_all, h) for h in heads]
    k = [head_cols(k_all, h) for h in heads]
    v = [head_cols(v_all, h) for h in heads]
    s = [s_ref[h] for h in heads]
    scores = [_dot_nt(q[h], k[h]) * decay_ref[h] for h in heads]
    cross = [_dot(q[h] * jnp.exp((pos + 1.0) * log_gamma[h]), s[h]) for h in heads]
    o = [_dot(scores[h], v[h]) + cross[h] for h in heads]
    kv = [_dot_tn(k[h] * jnp.exp((chunk - 1.0 - pos) * log_gamma[h]), v[h]) for h in heads]
    gain = nb_ref[...]
    for h in heads:
        s_ref[h] = s[h] * math.exp(chunk * log_gamma[h]) + kv[h]
        on = o[h] * lax.rsqrt(jnp.mean(o[h] * o[h], axis=-1, keepdims=True) + NORM_EPS) * gain
        o_ref[0, :, h * d:(h + 1) * d] = (on * head_cols(g_all, h)).astype(o_ref.dtype)


def _mixer_b(q, k, v, g, g_block, norm_b, n_heads):
    b, t, hd = q.shape
    d = B_HEAD_DIM
    chunk = min(B_CHUNK, t)
    hh = n_heads
    col = pl.BlockSpec((1, chunk, hd), lambda bi, ti: (bi, ti, 0))
    return pl.pallas_call(
        functools.partial(_mixer_b_kernel, chunk=chunk, n_heads=hh),
        out_shape=jax.ShapeDtypeStruct((b, t, hd), BF16),
        grid=(b, t // chunk),
        in_specs=[col, col, col, pl.BlockSpec((1, chunk, hd), lambda bi, ti: (bi, ti, g_block)),
                  pl.BlockSpec((1, d), lambda bi, ti: (0, 0))],
        out_specs=col,
        scratch_shapes=[pltpu.VMEM((hh, d, d), F32), pltpu.VMEM((hh, chunk, chunk), F32)],
        compiler_params=_params(("arbitrary", "arbitrary")),
        name="mixer_retention",
    )(q, k, v, g, norm_b.reshape(1, d).astype(F32))


def _mixer_c_kernel(q_ref, k_ref, v_ref, r_ref, lr_ref, wup_ref, bgk_ref, nc_ref, o_ref, st_ref, *, chunk, n_heads):
    heads = range(n_heads)

    @pl.when(pl.program_id(1) == 0)
    def _init():
        st_ref[...] = jnp.zeros_like(st_ref)

    dk = q_ref.shape[-1] // n_heads
    dv = v_ref.shape[-1] // n_heads
    q_all, k_all, v_all, r_all = q_ref[0], k_ref[0], v_ref[0], r_ref[0]

    logit = _dot(lr_ref[0], wup_ref[...]) + bgk_ref[...]
    gk = -_softplus(-logit) / GK_NORMALIZER
    gc_all = _split_dot(_chunk_cumsum_mask(chunk, chunk), gk, 2)
    mid = chunk // 2 - 1
    ii = lax.broadcasted_iota(jnp.int32, (chunk, chunk), 0)
    jj = lax.broadcasted_iota(jnp.int32, (chunk, chunk), 1)
    causal = ii >= jj

    q = [q_all[:, h * dk:(h + 1) * dk] for h in heads]
    k = [k_all[:, h * dk:(h + 1) * dk] for h in heads]
    v = [v_all[:, h * dv:(h + 1) * dv] for h in heads]
    gc = [gc_all[:, h * dk:(h + 1) * dk] for h in heads]
    g_mid = [gc[h][mid:mid + 1] for h in heads]
    g_last = [gc[h][chunk - 1:chunk] for h in heads]
    st = [st_ref[h] for h in heads]
    scores = [jnp.where(causal, _dot_nt(q[h] * jnp.exp(gc[h] - g_mid[h]), k[h] * jnp.exp(g_mid[h] - gc[h])), 0.0)
              for h in heads]
    cross = [_dot_nt(q[h] * jnp.exp(gc[h]), st[h]) for h in heads]
    o = [_dot(scores[h], v[h]) + cross[h] for h in heads]
    kv = [_dot_tn(v[h], k[h] * jnp.exp(g_last[h] - gc[h])) for h in heads]
    gain = nc_ref[...]
    for h in heads:
        st_ref[h] = st[h] * jnp.exp(g_last[h]) + kv[h]
        on = o[h] * lax.rsqrt(jnp.mean(o[h] * o[h], axis=-1, keepdims=True) + NORM_EPS) * gain
        o_ref[0, :, h * dv:(h + 1) * dv] = (on * r_all[:, h * dv:(h + 1) * dv]).astype(o_ref.dtype)


def _mixer_c(q, k, v, r, lr, w_up, b_gk, norm_c):
    b, t, kw = q.shape
    vw = v.shape[-1]
    hh = C_HEADS
    dk, dv = kw // hh, vw // hh
    chunk = min(C_CHUNK, t)
    rw = lr.shape[-1]
    kcol = pl.BlockSpec((1, chunk, kw), lambda bi, ti: (bi, ti, 0))
    vcol = pl.BlockSpec((1, chunk, vw), lambda bi, ti: (bi, ti, 0))
    return pl.pallas_call(
        functools.partial(_mixer_c_kernel, chunk=chunk, n_heads=hh),
        out_shape=jax.ShapeDtypeStruct((b, t, vw), BF16),
        grid=(b, t // chunk),
        in_specs=[kcol, kcol, vcol, vcol,
                  pl.BlockSpec((1, chunk, rw), lambda bi, ti: (bi, ti, 0)),
                  pl.BlockSpec((rw, kw), lambda bi, ti: (0, 0)),
                  pl.BlockSpec((1, kw), lambda bi, ti: (0, 0)),
                  pl.BlockSpec((1, dv), lambda bi, ti: (0, 0))],
        out_specs=vcol,
        scratch_shapes=[pltpu.VMEM((hh, dv, dk), F32)],
        compiler_params=_params(("parallel", "arbitrary")),
        name="mixer_gla",
    )(q, k, v, r, lr, w_up, b_gk.reshape(1, kw).astype(F32), norm_c.reshape(1, dv).astype(F32))


def _pad_cols(w, width):
    return jnp.pad(w, ((0, 0), (0, width - w.shape[1])))


def _deinterleave_heads(w, n_heads, head_dim):
    d = w.shape[0]
    return w.reshape(d, n_heads, head_dim // 2, 2).transpose(0, 1, 3, 2).reshape(d, n_heads * head_dim)


def _mlp(x2d, gain, w_up, w_down):
    hn = _rmsnorm(x2d, gain, BF16)
    hid = _matmul(hn, w_up.astype(BF16), act="relu2", out_dtype=BF16)
    return _matmul(hid, w_down.astype(BF16), res=x2d, tk=2048)


def _even_layer(x2d, b, t, gain, w_in, conv_w, a_log, dt_bias, norm_a, norm_b, w_out):
    d = x2d.shape[1]
    ha, hb = d // 256, d // 512
    akw = ha * A_HEAD_DIM
    bkw = hb * B_HEAD_DIM
    small0 = 4 * akw
    b0 = small0 + 2 * ha
    conv_w = conv_w.astype(F32)

    def w(lo, width):
        return w_in[:, lo:lo + width]

    def seq(a2d):
        return a2d.reshape(b, t, -1)

    hn = _rmsnorm(x2d, gain, BF16)
    q_a = seq(_conv_proj(hn, w(0, akw).astype(BF16), conv_w[:, :akw], t,
                         norm_dim=A_HEAD_DIM, scale=A_HEAD_DIM ** -0.5))
    k_a = seq(_conv_proj(hn, w(akw, akw).astype(BF16), conv_w[:, akw:2 * akw], t, norm_dim=A_HEAD_DIM))
    v_a = seq(_conv_proj(hn, w(2 * akw, akw).astype(BF16), conv_w[:, 2 * akw:], t))
    zg = seq(_matmul(hn, jnp.concatenate([w(3 * akw, akw), w(b0 + 3 * bkw, bkw)], axis=1).astype(BF16), act="silu"))
    ba = _matmul(hn, _pad_cols(w(small0, 2 * ha), LANES).astype(BF16))
    gates = _gates(ba, a_log, dt_bias, ha, A_CHUNK).reshape(b, t, LANES)
    gates_t = jnp.swapaxes(gates[:, :, ha:2 * ha], 1, 2)
    o_a = _mixer_a(q_a, k_a, v_a, zg, 0, gates, gates_t, norm_a, ha)
    cos, sin = _rope_tables(t, B_HEAD_DIM // 2)
    q_b = seq(_rope_proj(hn, _deinterleave_heads(w(b0, bkw), hb, B_HEAD_DIM).astype(BF16), cos, sin, t, B_HEAD_DIM))
    k_b = seq(_rope_proj(hn, _deinterleave_heads(w(b0 + bkw, bkw), hb, B_HEAD_DIM).astype(BF16), cos, sin, t,
                         B_HEAD_DIM, scale=B_HEAD_DIM ** -0.5))
    v_b = seq(_matmul(hn, w(b0 + 2 * bkw, bkw).astype(BF16)))
    o_b = _mixer_b(q_b, k_b, v_b, zg, 1, norm_b, hb)
    mixed = jnp.concatenate([o_a, o_b], axis=-1).reshape(b * t, -1)
    return _matmul(mixed, w_out.astype(BF16), res=x2d, tn=512)


def _odd_layer(x2d, b, t, gain, w_in, w_gk_down, w_gk_up, b_gk, norm_c, w_out):
    d = x2d.shape[1]
    kw, vw = d // 2, d
    dk = kw // C_HEADS

    def seq(a2d):
        return a2d.reshape(b, t, -1)

    hn = _rmsnorm(x2d, gain, BF16)
    q = seq(_matmul(hn, w_in[:, :kw].astype(BF16), act=dk ** -0.5))
    k = seq(_matmul(hn, w_in[:, kw:2 * kw].astype(BF16)))
    v = seq(_matmul(hn, w_in[:, 2 * kw:2 * kw + vw].astype(BF16)))
    r = seq(_matmul(hn, w_in[:, 2 * kw + vw:].astype(BF16), act="silu"))
    lr = seq(_matmul(hn, _pad_cols(w_gk_down, LANES).astype(BF16)))
    w_up = jnp.pad(w_gk_up, ((0, LANES - w_gk_up.shape[0]), (0, 0))).astype(BF16)
    o_c = _mixer_c(q, k, v, r, lr, w_up, b_gk, norm_c).reshape(b * t, vw)
    return _matmul(o_c, w_out.astype(BF16), res=x2d, tn=512)


def kernel(x, norm_mix, norm_mlp, norm_final, w_up, w_down, w_in_ab, conv_a, a_log, dt_bias, norm_a, norm_b,
           w_out_ab, w_in_c, w_gk_down, w_gk_up, b_gk, norm_c, w_out_c):
    b, t, d = x.shape
    depth = norm_mix.shape[0]
    x2d = x.reshape(b * t, d)
    for layer in range(depth):
        i = layer // 2
        if layer % 2 == 0:
            x2d = _even_layer(x2d, b, t, norm_mix[layer], w_in_ab[i], conv_a[i], a_log[i], dt_bias[i],
                              norm_a[i], norm_b[i], w_out_ab[i])
        else:
            x2d = _odd_layer(x2d, b, t, norm_mix[layer], w_in_c[i], w_gk_down[i], w_gk_up[i], b_gk[i],
                             norm_c[i], w_out_c[i])
        x2d = _mlp(x2d, norm_mlp[layer], w_up[layer], w_down[layer])
    return _rmsnorm(x2d, norm_final, F32).reshape(b, t, d)
```

```python
import functools
import math

import jax
import jax.numpy as jnp
from jax import lax
from jax.experimental import pallas as pl
from jax.experimental.pallas import tpu as pltpu

F32 = jnp.float32
BF16 = jnp.bfloat16

NORM_EPS = 1e-6
ROPE_BASE = 10000.0
CONV_K = 4
GK_RANK = 16
GK_NORMALIZER = 16.0
A_HEAD_DIM = 128
B_HEAD_DIM = 256
C_HEADS = 4
LANES = 128
SUBLANES = 8
MXU_COLS = 256
CONV_PROJ_ROWS = 512
A_CHUNK = 128
B_CHUNK = 256
C_CHUNK = 256
C_SUBCHUNK = 128
MIB = 1024 * 1024


def _params(semantics, vmem_mib=None):
    kwargs = dict(dimension_semantics=semantics)
    if vmem_mib is not None:
        kwargs["vmem_limit_bytes"] = vmem_mib * MIB
    return pltpu.CompilerParams(**kwargs)


def _dot(a, b):
    return jnp.dot(a.astype(BF16), b.astype(BF16), preferred_element_type=F32)


def _dot_nt(a, b):
    return lax.dot_general(a.astype(BF16), b.astype(BF16), (((1,), (1,)), ((), ())),
                           preferred_element_type=F32)


def _dot_tn(a, b):
    return lax.dot_general(a.astype(BF16), b.astype(BF16), (((0,), (0,)), ((), ())),
                           preferred_element_type=F32)


def _sigmoid(x):
    return 1.0 / (1.0 + jnp.exp(-x))


def _silu(x):
    return x * _sigmoid(x)


def _softplus(x):
    return jnp.maximum(x, 0.0) + jnp.log1p(jnp.exp(-jnp.abs(x)))


def _split_dot(mask_bf16, g, pieces):
    acc = None
    rem = g
    for _ in range(pieces):
        part = rem.astype(BF16)
        term = jnp.dot(mask_bf16, part, preferred_element_type=F32)
        acc = term if acc is None else acc + term
        rem = rem - part.astype(F32)
    return acc


def _chunk_cumsum_mask(n, chunk):
    ii = lax.broadcasted_iota(jnp.int32, (n, n), 0)
    jj = lax.broadcasted_iota(jnp.int32, (n, n), 1)
    same = (ii // chunk) == (jj // chunk)
    return jnp.where(same & (ii >= jj), 1.0, 0.0).astype(BF16)


def _rmsnorm_kernel(x_ref, g_ref, o_ref):
    x = x_ref[...]
    ms = jnp.mean(x * x, axis=-1, keepdims=True)
    o_ref[...] = (x * lax.rsqrt(ms + NORM_EPS) * g_ref[...]).astype(o_ref.dtype)


def _rmsnorm(x2d, gain, out_dtype, tm=256):
    m, d = x2d.shape
    tm = min(tm, m)
    return pl.pallas_call(
        _rmsnorm_kernel,
        out_shape=jax.ShapeDtypeStruct((m, d), out_dtype),
        grid=(m // tm,),
        in_specs=[pl.BlockSpec((tm, d), lambda i: (i, 0)),
                  pl.BlockSpec((1, d), lambda i: (0, 0))],
        out_specs=pl.BlockSpec((tm, d), lambda i: (i, 0)),
        compiler_params=_params(("parallel",)),
        name="rmsnorm",
    )(x2d, gain.reshape(1, d).astype(F32))


def _matmul_kernel(*refs, nk, n_pairs, act, has_res):
    ab_refs = refs[:2 * n_pairs]
    res_ref = refs[2 * n_pairs] if has_res else None
    o_ref = refs[2 * n_pairs + int(has_res)]

    def product():
        acc = None
        for p in range(n_pairs):
            term = jnp.dot(ab_refs[2 * p][...], ab_refs[2 * p + 1][...], preferred_element_type=F32)
            acc = term if acc is None else acc + term
        return acc

    if nk == 1:
        acc = product()
        if act == "relu2":
            r = jnp.maximum(acc, 0.0)
            acc = r * r
        elif act == "silu":
            acc = _silu(acc)
        elif isinstance(act, float):
            acc = acc * act
        if has_res:
            acc = res_ref[...] + acc
        o_ref[...] = acc.astype(o_ref.dtype)
    else:
        @pl.when(pl.program_id(2) == 0)
        def _first():
            o_ref[...] = res_ref[...] if has_res else jnp.zeros_like(o_ref)

        o_ref[...] += product()


def _matmul(a, b, *, layer=None, cols=None, a2=None, res=None, act=None, out_dtype=F32, tm=1024, tn=1024,
            tk=None, vmem_mib=56):
    m, ka = a.shape
    col0, n = (0, b.shape[-1]) if cols is None else cols
    n_pairs = 1 if a2 is None else 2
    tm, tn = min(tm, m), min(tn, n)
    tk = ka if tk is None else min(tk, ka)
    assert m % tm == 0 and n % tn == 0 and ka % tk == 0 and col0 % tn == 0
    jb0 = col0 // tn
    nk = ka // tk
    assert nk == 1 or (act is None and out_dtype == F32 and a2 is None)
    has_res = res is not None
    stacked = layer is not None

    def a_map(i, j, *k):
        return (i, k[0] if k else 0)

    def o_map(i, j, *k):
        return (i, j)

    def b_spec(row_block):
        def b_map(i, j, *k):
            kb = (k[0] if k else 0) + row_block
            return (layer, kb, j + jb0) if stacked else (kb, j + jb0)
        return pl.BlockSpec((None, tk, tn) if stacked else (tk, tn), b_map)

    a_spec = pl.BlockSpec((tm, tk), a_map)
    o_spec = pl.BlockSpec((tm, tn), o_map)
    in_specs = [a_spec, b_spec(0)]
    args = [a, b]
    if a2 is not None:
        in_specs += [a_spec, b_spec(1)]
        args += [a2, b]
    if has_res:
        in_specs.append(o_spec)
        args.append(res)
    grid = (m // tm, n // tn) + ((nk,) if nk > 1 else ())
    sem = ("parallel", "parallel") + (("arbitrary",) if nk > 1 else ())
    return pl.pallas_call(
        functools.partial(_matmul_kernel, nk=nk, n_pairs=n_pairs, act=act, has_res=has_res),
        out_shape=jax.ShapeDtypeStruct((m, n), out_dtype),
        grid=grid,
        in_specs=in_specs,
        out_specs=o_spec,
        compiler_params=_params(sem, vmem_mib),
        name="matmul",
    )(*args)


def _conv_proj_kernel(a_ref, b_ref, cw_ref, o_ref, carry_ref, raw_ref, *, tiles_per_seq, norm_dim, scale):
    i, j = pl.program_id(0), pl.program_id(1)
    tm, tn = o_ref.shape
    rows_per_dot = min(CONV_PROJ_ROWS, tm)
    first_tile = lax.rem(i, tiles_per_seq) == 0

    def epilogue(acc, r0, c0):
        cols = slice(c0, c0 + MXU_COLS)
        if r0 == 0:
            prev = carry_ref[j, :, cols]
            prev = jnp.where(first_tile, jnp.zeros_like(prev), prev)
        else:
            prev = raw_ref[r0 - SUBLANES:r0, cols]
        if r0 + rows_per_dot == tm:
            carry_ref[j, :, cols] = acc[rows_per_dot - SUBLANES:]
        ext = jnp.concatenate([prev, acc], axis=0)
        w = cw_ref[:, cols]
        y = w[CONV_K - 1:CONV_K] * acc
        for tap in range(CONV_K - 1):
            off = SUBLANES - (CONV_K - 1) + tap
            y = y + w[tap:tap + 1] * ext[off:off + rows_per_dot]
        y = _silu(y)
        if norm_dim is not None:
            segs = []
            for s0 in range(0, MXU_COLS, norm_dim):
                seg = y[:, s0:s0 + norm_dim]
                segs.append(seg * (lax.rsqrt(jnp.sum(seg * seg, axis=-1, keepdims=True) + NORM_EPS) * scale))
            y = jnp.concatenate(segs, axis=1)
        o_ref[r0:r0 + rows_per_dot, cols] = y.astype(o_ref.dtype)

    subtiles = [(r0, c0) for r0 in range(0, tm, rows_per_dot) for c0 in range(0, tn, MXU_COLS)]
    pending = None
    for r0, c0 in subtiles:
        raw_ref[r0:r0 + rows_per_dot, c0:c0 + MXU_COLS] = jnp.dot(
            a_ref[r0:r0 + rows_per_dot, :], b_ref[:, c0:c0 + MXU_COLS], preferred_element_type=F32)
        if pending is not None:
            pr, pc = pending
            epilogue(raw_ref[pr:pr + rows_per_dot, pc:pc + MXU_COLS], pr, pc)
        pending = (r0, c0)
    pr, pc = pending
    epilogue(raw_ref[pr:pr + rows_per_dot, pc:pc + MXU_COLS], pr, pc)


def _conv_proj(a, b, conv_w, cols, seq_len, *, norm_dim=None, scale=1.0, tm=1024, tn=1024, vmem_mib=56):
    m, kdim = a.shape
    col0, n = cols
    tm, tn = min(tm, seq_len), min(tn, n)
    assert m % tm == 0 and n % tn == 0 and seq_len % tm == 0 and col0 % tn == 0
    jb0 = col0 // tn
    return pl.pallas_call(
        functools.partial(_conv_proj_kernel, tiles_per_seq=seq_len // tm, norm_dim=norm_dim, scale=scale),
        out_shape=jax.ShapeDtypeStruct((m, n), F32),
        grid=(m // tm, n // tn),
        in_specs=[pl.BlockSpec((tm, kdim), lambda i, j: (i, 0)),
                  pl.BlockSpec((kdim, tn), lambda i, j: (0, j + jb0)),
                  pl.BlockSpec((CONV_K, tn), lambda i, j: (0, j + jb0))],
        out_specs=pl.BlockSpec((tm, tn), lambda i, j: (i, j)),
        scratch_shapes=[pltpu.VMEM((n // tn, SUBLANES, tn), F32), pltpu.VMEM((tm, tn), F32)],
        compiler_params=_params(("arbitrary", "arbitrary"), vmem_mib),
        name="conv_proj",
    )(a, b, conv_w)


def _rope_proj_kernel(a_ref, b_ref, cos_ref, sin_ref, o_ref, *, head_dim, scale):
    acc = jnp.dot(a_ref[...], b_ref[...], preferred_element_type=F32)
    cos, sin = cos_ref[...], sin_ref[...]
    half = head_dim // 2
    parts = []
    for c0 in range(0, acc.shape[1], head_dim):
        x1, x2 = acc[:, c0:c0 + half], acc[:, c0 + half:c0 + head_dim]
        parts += [(x1 * cos - x2 * sin) * scale, (x1 * sin + x2 * cos) * scale]
    o_ref[...] = jnp.concatenate(parts, axis=1).astype(o_ref.dtype)


def _rope_proj(a, b, cos, sin, seq_len, head_dim, *, scale=1.0, tm=1024, tn=1024, vmem_mib=56):
    m, kdim = a.shape
    n = b.shape[1]
    tm, tn = min(tm, seq_len), min(tn, n)
    assert m % tm == 0 and n % tn == 0 and seq_len % tm == 0 and tn % head_dim == 0
    tiles_per_seq = seq_len // tm
    tab = pl.BlockSpec((tm, head_dim // 2), lambda i, j: (i % tiles_per_seq, 0))
    return pl.pallas_call(
        functools.partial(_rope_proj_kernel, head_dim=head_dim, scale=scale),
        out_shape=jax.ShapeDtypeStruct((m, n), F32),
        grid=(m // tm, n // tn),
        in_specs=[pl.BlockSpec((tm, kdim), lambda i, j: (i, 0)),
                  pl.BlockSpec((kdim, tn), lambda i, j: (0, j)),
                  tab, tab],
        out_specs=pl.BlockSpec((tm, tn), lambda i, j: (i, j)),
        compiler_params=_params(("parallel", "parallel"), vmem_mib),
        name="rope_proj",
    )(a, b, cos, sin)


def _rope_kernel(inv_ref, cos_ref, sin_ref, *, tt):
    pos = (lax.broadcasted_iota(jnp.int32, cos_ref.shape, 0) + pl.program_id(0) * tt).astype(F32)
    ang = pos * inv_ref[...]
    cos_ref[...] = jnp.cos(ang)
    sin_ref[...] = jnp.sin(ang)


def _rope_tables(t, half, tt=512):
    tt = min(tt, t)
    inv_freq = jnp.power(ROPE_BASE, -jnp.linspace(0.0, 1.0, half, dtype=F32)).reshape(1, half)
    return pl.pallas_call(
        functools.partial(_rope_kernel, tt=tt),
        out_shape=(jax.ShapeDtypeStruct((t, half), F32), jax.ShapeDtypeStruct((t, half), F32)),
        grid=(t // tt,),
        in_specs=[pl.BlockSpec((1, half), lambda i: (0, 0))],
        out_specs=(pl.BlockSpec((tt, half), lambda i: (i, 0)), pl.BlockSpec((tt, half), lambda i: (i, 0))),
        compiler_params=_params(("parallel",)),
        name="rope_tables",
    )(inv_freq)


def _gates_kernel(x_ref, alog_ref, dtb_ref, o_ref, *, n_heads, chunk):
    x = x_ref[...]
    beta = _sigmoid(x)
    g = -jnp.exp(alog_ref[...]) * _softplus(x + dtb_ref[...])
    gc = _split_dot(_chunk_cumsum_mask(x.shape[0], chunk), g, 3)
    lane = lax.broadcasted_iota(jnp.int32, x.shape, 1)
    o_ref[...] = jnp.where(lane < n_heads, beta, gc)


def _gates(ba, a_log, dt_bias, n_heads, chunk, tg=256):
    m, w = ba.shape
    tg = min(tg, m)
    alog_p = jnp.zeros((1, w), F32).at[0, n_heads:2 * n_heads].set(a_log.astype(F32))
    dtb_p = jnp.zeros((1, w), F32).at[0, n_heads:2 * n_heads].set(dt_bias.astype(F32))
    return pl.pallas_call(
        functools.partial(_gates_kernel, n_heads=n_heads, chunk=chunk),
        out_shape=jax.ShapeDtypeStruct((m, w), F32),
        grid=(m // tg,),
        in_specs=[pl.BlockSpec((tg, w), lambda i: (i, 0)),
                  pl.BlockSpec((1, w), lambda i: (0, 0)),
                  pl.BlockSpec((1, w), lambda i: (0, 0))],
        out_specs=pl.BlockSpec((tg, w), lambda i: (i, 0)),
        compiler_params=_params(("parallel",)),
        name="deltanet_gates",
    )(ba, alog_p, dtb_p)


def _unit_lower_inverses(mats, n):
    ii = lax.broadcasted_iota(jnp.int32, (n, n), 0)
    jj = lax.broadcasted_iota(jnp.int32, (n, n), 1)
    base = 16
    eye = jnp.where(ii == jj, 1.0, 0.0)
    diag_blocks = (ii // base) == (jj // base)
    ps = [jnp.where(diag_blocks, -a, 0.0) for a in mats]
    ts = [eye + p for p in ps]
    width = 2
    while width < base:
        ps = [_dot(p, p) for p in ps]
        ts = [t + _dot(t, p) for t, p in zip(ts, ps)]
        width *= 2
    bs = base
    while bs < n:
        off_blocks = ((ii // (2 * bs)) == (jj // (2 * bs))) & ((ii // bs) != (jj // bs))
        xs = [_dot(t, jnp.where(off_blocks, a, 0.0)) for t, a in zip(ts, mats)]
        ts = [t - _dot(x, t) for t, x in zip(ts, xs)]
        bs *= 2
    return ts


def _mixer_a_kernel(q_ref, k_ref, v_ref, z_ref, gate_ref, gct_ref, na_ref, o_ref, s_ref, *, tb, n_heads):
    d = A_HEAD_DIM
    heads = range(n_heads)

    @pl.when(pl.program_id(1) == 0)
    def _init():
        s_ref[...] = jnp.zeros_like(s_ref)

    q_all, k_all, v_all, z_all = q_ref[0], k_ref[0], v_ref[0], z_ref[0]
    gt = gate_ref[0]
    lane = lax.broadcasted_iota(jnp.int32, gt.shape, 1)

    ii = lax.broadcasted_iota(jnp.int32, (tb, tb), 0)
    jj = lax.broadcasted_iota(jnp.int32, (tb, tb), 1)
    causal = ii >= jj
    strict = ii > jj
    gain = na_ref[...]

    def head_cols(x, h):
        return x[:, h * d:(h + 1) * d]

    q = [head_cols(q_all, h) for h in heads]
    k = [head_cols(k_all, h) for h in heads]
    beta =[jnp.sum(jnp.where(lane == h, gt, 0.0), axis=1, keepdims=True) for h in heads]
    gc = [jnp.sum(jnp.where(lane == h + n_heads, gt, 0.0), axis=1, keepdims=True) for h in heads]
    grow = gct_ref[0]
    decay = [jnp.where(causal, jnp.exp(jnp.where(causal, gc[h] - grow[h:h + 1], 0.0)), 0.0) for h in heads]
    kb = [k[h] * beta[h] for h in heads]
    eg = [jnp.exp(gc[h]) for h in heads]

    kq = [_dot_nt(jnp.concatenate([kb[h], q[h]], axis=0), k[h]) for h in heads]
    a = [jnp.where(strict, kq[h][:tb] * decay[h], 0.0) for h in heads]
    scores = [kq[h][tb:] * decay[h] for h in heads]
    t = _unit_lower_inverses(a, tb)
    uw = [_dot(t[h], jnp.concatenate([head_cols(v_all, h) * beta[h], kb[h] * eg[h]], axis=1)) for h in heads]
    s = [s_ref[h] for h in heads]
    ws = [_dot(jnp.concatenate([uw[h][:, d:], q[h] * eg[h]], axis=0), s[h]) for h in heads]
    v_new = [uw[h][:, :d] - ws[h][:tb] for h in heads]
    o = [ws[h][tb:] + _dot(scores[h], v_new[h]) for h in heads]
    g_last = [gc[h][tb - 1:tb] for h in heads]
    kv = [_dot_tn(k[h] * jnp.exp(g_last[h] - gc[h]), v_new[h]) for h in heads]
    for h in heads:
        s_ref[h] = s[h] * jnp.exp(g_last[h]) + kv[h]
        on = o[h] * lax.rsqrt(jnp.mean(o[h] * o[h], axis=-1, keepdims=True) + NORM_EPS) * gain
        o_ref[0, :, h * d:(h + 1) * d] = (on * head_cols(z_all, h)).astype(o_ref.dtype)


def _mixer_a(q, k, v, z, gates, gates_t, norm_a, n_heads):
    b, t, hd = q.shape
    d = A_HEAD_DIM
    tb = min(A_CHUNK, t)
    hh = n_heads
    col = pl.BlockSpec((1, tb, hd), lambda bi, ti: (bi, ti, 0))
    return pl.pallas_call(
        functools.partial(_mixer_a_kernel, tb=tb, n_heads=hh),
        out_shape=jax.ShapeDtypeStruct((b, t, hd), BF16),
        grid=(b, t // tb),
        in_specs=[col, col, col, col,
                  pl.BlockSpec((1, tb, gates.shape[-1]), lambda bi, ti: (bi, ti, 0)),
                  pl.BlockSpec((1, hh, tb), lambda bi, ti: (bi, 0, ti)),
                  pl.BlockSpec((1, d), lambda bi, ti: (0, 0))],
        out_specs=col,
        scratch_shapes=[pltpu.VMEM((hh, d, d), F32)],
        compiler_params=_params(("parallel", "arbitrary")),
        name="mixer_deltanet",
    )(q, k, v, z, gates, gates_t, norm_a.reshape(1, d).astype(F32))


def _mixer_b_kernel(q_ref, k_ref, v_ref, g_ref, nb_ref, o_ref, s_ref, decay_ref, *, chunk, n_heads):
    d = B_HEAD_DIM
    heads = range(n_heads)
    log_gamma = [math.log1p(-(2.0 ** (-5.0 - h))) for h in heads]

    @pl.when(pl.program_id(1) == 0)
    def _init():
        s_ref[...] = jnp.zeros_like(s_ref)
        ii = lax.broadcasted_iota(jnp.int32, (chunk, chunk), 0)
        jj = lax.broadcasted_iota(jnp.int32, (chunk, chunk), 1)
        causal = ii >= jj
        dist = jnp.where(causal, ii - jj, 0).astype(F32)
        for h in heads:
            decay_ref[h] = jnp.where(causal, jnp.exp(dist * log_gamma[h]), 0.0)

    pos = lax.broadcasted_iota(jnp.int32, (chunk, 1), 0).astype(F32)
    q_all, k_all, v_all, g_all = q_ref[0], k_ref[0], v_ref[0], g_ref[0]

    def head_cols(x, h):
        return x[:, h * d:(h + 1) * d]

    q = [head_cols(q_all, h) for h in heads]
    k = [head_cols(k_all, h) for h in heads]
    v = [head_cols(v_all, h) for h in heads]
    s = [s_ref[h] for h in heads]
    scores = [_dot_nt(q[h], k[h]) * decay_ref[h] for h in heads]
    cross = [_dot(q[h] * jnp.exp((pos + 1.0) * log_gamma[h]), s[h]) for h in heads]
    o = [_dot(scores[h], v[h]) + cross[h] for h in heads]
    kv = [_dot_tn(k[h] * jnp.exp((chunk - 1.0 - pos) * log_gamma[h]), v[h]) for h in heads]
    gain = nb_ref[...]
    for h in heads:
        s_ref[h] = s[h] * math.exp(chunk * log_gamma[h]) + kv[h]
        on = o[h] * lax.rsqrt(jnp.mean(o[h] * o[h], axis=-1, keepdims=True) + NORM_EPS) * gain
        o_ref[0, :, h * d:(h + 1) * d] = (on * head_cols(g_all, h)).astype(o_ref.dtype)


def _mixer_b(q, k, v, g, norm_b, n_heads):
    b, t, hd = q.shape
    d = B_HEAD_DIM
    chunk = min(B_CHUNK, t)
    hh = n_heads
    col = pl.BlockSpec((1, chunk, hd), lambda bi, ti: (bi, ti, 0))
    return pl.pallas_call(
        functools.partial(_mixer_b_kernel, chunk=chunk, n_heads=hh),
        out_shape=jax.ShapeDtypeStruct((b, t, hd), BF16),
        grid=(b, t // chunk),
        in_specs=[col, col, col, col, pl.BlockSpec((1, d), lambda bi, ti: (0, 0))],
        out_specs=col,
        scratch_shapes=[pltpu.VMEM((hh, d, d), F32), pltpu.VMEM((hh, chunk, chunk), F32)],
        compiler_params=_params(("arbitrary", "arbitrary")),
        name="mixer_retention",
    )(q, k, v, g, norm_b.reshape(1, d).astype(F32))


def _mixer_c_kernel(q_ref, k_ref, v_ref, r_ref, lr_ref, wup_ref, bgk_ref, nc_ref, o_ref, st_ref, *, chunk, n_heads):
    heads = range(n_heads)

    @pl.when(pl.program_id(1) == 0)
    def _init():
        st_ref[...] = jnp.zeros_like(st_ref)

    dk = q_ref.shape[-1] // n_heads
    dv = v_ref.shape[-1] // n_heads
    q_all, k_all, v_all, r_all = q_ref[0], k_ref[0], v_ref[0], r_ref[0]

    logit = _dot(lr_ref[0], wup_ref[...]) + bgk_ref[...]
    gk = -_softplus(-logit) / GK_NORMALIZER
    gc_all = _split_dot(_chunk_cumsum_mask(chunk, chunk), gk, 2)
    sub = min(C_SUBCHUNK, chunk)
    ii = lax.broadcasted_iota(jnp.int32, (sub, sub), 0)
    jj = lax.broadcasted_iota(jnp.int32, (sub, sub), 1)
    causal = ii >= jj

    q = [q_all[:, h * dk:(h + 1) * dk] for h in heads]
    k = [k_all[:, h * dk:(h + 1) * dk] for h in heads]
    v = [v_all[:, h * dv:(h + 1) * dv] for h in heads]
    gc = [gc_all[:, h * dk:(h + 1) * dk] for h in heads]
    g_last = [gc[h][chunk - 1:chunk] for h in heads]
    st = [st_ref[h] for h in heads]

    def score_rows(h, r0):
        rows = slice(r0, r0 + sub)
        mid = gc[h][r0 + sub // 2 - 1:r0 + sub // 2]
        diag = jnp.where(causal, _dot_nt(q[h][rows] * jnp.exp(gc[h][rows] - mid),
                                         k[h][rows] * jnp.exp(mid - gc[h][rows])), 0.0)
        parts = [diag]
        if r0 > 0:
            bnd = gc[h][r0 - 1:r0]
            parts.insert(0, _dot_nt(q[h][rows] * jnp.exp(gc[h][rows] - bnd), k[h][:r0] * jnp.exp(bnd - gc[h][:r0])))
        if r0 + sub < chunk:
            parts.append(jnp.zeros((sub, chunk - r0 - sub), F32))
        return jnp.concatenate(parts, axis=1) if len(parts) > 1 else diag

    score_blocks = [[score_rows(h, r0) for h in heads] for r0 in range(0, chunk, sub)]
    scores = [jnp.concatenate([blk[h] for blk in score_blocks], axis=0) if len(score_blocks) > 1
              else score_blocks[0][h] for h in heads]
    cross = [_dot_nt(q[h] * jnp.exp(gc[h]), st[h]) for h in heads]
    o = [_dot(scores[h], v[h]) + cross[h] for h in heads]
    kv = [_dot_tn(v[h], k[h] * jnp.exp(g_last[h] - gc[h])) for h in heads]
    gain = nc_ref[...]
    for h in heads:
        st_ref[h] = st[h] * jnp.exp(g_last[h]) + kv[h]
        on = o[h] * lax.rsqrt(jnp.mean(o[h] * o[h], axis=-1, keepdims=True) + NORM_EPS) * gain
        o_ref[0, :, h * dv:(h + 1) * dv] = (on * r_all[:, h * dv:(h + 1) * dv]).astype(o_ref.dtype)


def _mixer_c(q, k, v, r, lr, w_up, b_gk, norm_c):
    b, t, kw = q.shape
    vw = v.shape[-1]
    hh = C_HEADS
    dk, dv = kw // hh, vw // hh
    chunk = min(C_CHUNK, t)
    rw = lr.shape[-1]
    kcol = pl.BlockSpec((1, chunk, kw), lambda bi, ti: (bi, ti, 0))
    vcol = pl.BlockSpec((1, chunk, vw), lambda bi, ti: (bi, ti, 0))
    return pl.pallas_call(
        functools.partial(_mixer_c_kernel, chunk=chunk, n_heads=hh),
        out_shape=jax.ShapeDtypeStruct((b, t, vw), BF16),
        grid=(b, t // chunk),
        in_specs=[kcol, kcol, vcol, vcol,
                  pl.BlockSpec((1, chunk, rw), lambda bi, ti: (bi, ti, 0)),
                  pl.BlockSpec((rw, kw), lambda bi, ti: (0, 0)),
                  pl.BlockSpec((1, kw), lambda bi, ti: (0, 0)),
                  pl.BlockSpec((1, dv), lambda bi, ti: (0, 0))],
        out_specs=vcol,
        scratch_shapes=[pltpu.VMEM((hh, dv, dk), F32)],
        compiler_params=_params(("parallel", "arbitrary")),
        name="mixer_gla",
    )(q, k, v, r, lr, w_up, b_gk.reshape(1, kw).astype(F32), norm_c.reshape(1, dv).astype(F32))


def _pad_cols(w, width):
    return jnp.pad(w, ((0, 0), (0, width - w.shape[1])))


def _deinterleave_heads(w, n_heads, head_dim):
    d = w.shape[0]
    return w.reshape(d, n_heads, head_dim // 2, 2).transpose(0, 1, 3, 2).reshape(d, n_heads * head_dim)


def _mlp(x2d, gain, w_up, w_down, layer):
    hn = _rmsnorm(x2d, gain, BF16)
    hid = _matmul(hn, w_up, layer=layer, act="relu2", out_dtype=BF16)
    return _matmul(hid, w_down, layer=layer, res=x2d, tk=4096)


def _even_layer(x2d, b, t, gain, w_in, conv_w, a_log, dt_bias, norm_a, norm_b, w_out):
    d = x2d.shape[1]
    ha, hb = d // 256, d // 512
    akw = ha * A_HEAD_DIM
    bkw = hb * B_HEAD_DIM
    small0 = 4 * akw
    b0 = small0 + 2 * ha
    conv_w = conv_w.astype(F32)

    def seq(a2d):
        return a2d.reshape(b, t, -1)

    hn = _rmsnorm(x2d, gain, BF16)
    w_bf = w_in.astype(BF16)
    q_a = seq(_conv_proj(hn, w_bf, conv_w, (0, akw), t, norm_dim=A_HEAD_DIM, scale=A_HEAD_DIM ** -0.5))
    k_a = seq(_conv_proj(hn, w_bf, conv_w, (akw, akw), t, norm_dim=A_HEAD_DIM))
    v_a = seq(_conv_proj(hn, w_bf, conv_w, (2 * akw, akw), t))
    z_a = seq(_matmul(hn, w_bf, cols=(3 * akw, akw), act="silu"))
    ba = _matmul(hn, _pad_cols(w_bf[:, small0:b0], LANES))
    gates = _gates(ba, a_log, dt_bias, ha, A_CHUNK).reshape(b, t, LANES)
    gates_t = jnp.swapaxes(gates[:, :, ha:2 * ha], 1, 2)
    o_a = _mixer_a(q_a, k_a, v_a, z_a, gates, gates_t, norm_a, ha)
    cos, sin = _rope_tables(t, B_HEAD_DIM // 2)
    q_b = seq(_rope_proj(hn, _deinterleave_heads(w_bf[:, b0:b0 + bkw], hb, B_HEAD_DIM), cos, sin, t, B_HEAD_DIM))
    k_b = seq(_rope_proj(hn, _deinterleave_heads(w_bf[:, b0 + bkw:b0 + 2 * bkw], hb, B_HEAD_DIM), cos, sin, t,
                         B_HEAD_DIM, scale=B_HEAD_DIM ** -0.5))
    v_b = seq(_matmul(hn, w_bf[:, b0 + 2 * bkw:b0 + 3 * bkw]))
    g_b = seq(_matmul(hn, w_bf[:, b0 + 3 * bkw:], act="silu"))
    o_b = _mixer_b(q_b, k_b, v_b, g_b, norm_b, hb)
    return _matmul(o_a.reshape(b * t, akw), w_out.astype(BF16), a2=o_b.reshape(b * t, bkw), res=x2d)


def _odd_layer(x2d, b, t, gain, w_in, w_gk_down, w_gk_up, b_gk, norm_c, w_out):
    d = x2d.shape[1]
    kw, vw = d // 2, d
    dk = kw // C_HEADS

    def seq(a2d):
        return a2d.reshape(b, t, -1)

    hn = _rmsnorm(x2d, gain, BF16)
    w_bf = w_in.astype(BF16)
    q = seq(_matmul(hn, w_bf, cols=(0, kw), act=dk ** -0.5))
    k = seq(_matmul(hn, w_bf, cols=(kw, kw)))
    v = seq(_matmul(hn, w_bf, cols=(2 * kw, vw)))
    r = seq(_matmul(hn, w_bf, cols=(2 * kw + vw, vw), act="silu"))
    lr = seq(_matmul(hn, _pad_cols(w_gk_down, LANES).astype(BF16)))
    w_up = jnp.pad(w_gk_up, ((0, LANES - w_gk_up.shape[0]), (0, 0))).astype(BF16)
    o_c = _mixer_c(q, k, v, r, lr, w_up, b_gk, norm_c).reshape(b * t, vw)
    return _matmul(o_c, w_out.astype(BF16), res=x2d)


def kernel(x, norm_mix, norm_mlp, norm_final, w_up, w_down, w_in_ab, conv_a, a_log, dt_bias, norm_a, norm_b,
           w_out_ab, w_in_c, w_gk_down, w_gk_up, b_gk, norm_c, w_out_c):
    b, t, d = x.shape
    depth = norm_mix.shape[0]
    x2d = x.reshape(b * t, d)
    w_up, w_down = w_up.astype(BF16), w_down.astype(BF16)
    for layer in range(depth):
        i = layer // 2
        if layer % 2 == 0:
            x2d = _even_layer(x2d, b, t, norm_mix[layer], w_in_ab[i], conv_a[i], a_log[i], dt_bias[i],
                              norm_a[i], norm_b[i], w_out_ab[i])
        else:
            x2d = _odd_layer(x2d, b, t, norm_mix[layer], w_in_c[i], w_gk_down[i], w_gk_up[i], b_gk[i],
                             norm_c[i], w_out_c[i])
        x2d = _mlp(x2d, norm_mlp[layer], w_up, w_down, layer)
    return _rmsnorm(x2d, norm_final, F32).reshape(b, t, d)
```

```python
import functools
import math

import jax
import jax.numpy as jnp
from jax import lax
from jax.experimental import pallas as pl
from jax.experimental.pallas import tpu as pltpu

F32 = jnp.float32
BF16 = jnp.bfloat16

NORM_EPS = 1e-6
ROPE_BASE = 10000.0
CONV_K = 4
GK_RANK = 16
GK_NORMALIZER = 16.0
A_HEAD_DIM = 128
B_HEAD_DIM = 256
C_HEADS = 4
LANES = 128
SUBLANES = 8
MXU_COLS = 256
CONV_PROJ_ROWS = 512
A_CHUNK = 128
B_CHUNK = 256
C_CHUNK = 256
C_SUBCHUNK = 128
MIB = 1024 * 1024


def _params(semantics, vmem_mib=None):
    kwargs = dict(dimension_semantics=semantics)
    if vmem_mib is not None:
        kwargs["vmem_limit_bytes"] = vmem_mib * MIB
    return pltpu.CompilerParams(**kwargs)


def _dot(a, b):
    return jnp.dot(a.astype(BF16), b.astype(BF16), preferred_element_type=F32)


def _dot_nt(a, b):
    return lax.dot_general(a.astype(BF16), b.astype(BF16), (((1,), (1,)), ((), ())),
                           preferred_element_type=F32)


def _dot_tn(a, b):
    return lax.dot_general(a.astype(BF16), b.astype(BF16), (((0,), (0,)), ((), ())),
                           preferred_element_type=F32)


def _sigmoid(x):
    return 1.0 / (1.0 + jnp.exp(-x))


def _silu(x):
    return x * _sigmoid(x)


def _softplus(x):
    return jnp.maximum(x, 0.0) + jnp.log1p(jnp.exp(-jnp.abs(x)))


def _split_dot(mask_bf16, g, pieces):
    acc = None
    rem = g
    for _ in range(pieces):
        part = rem.astype(BF16)
        term = jnp.dot(mask_bf16, part, preferred_element_type=F32)
        acc = term if acc is None else acc + term
        rem = rem - part.astype(F32)
    return acc


def _chunk_cumsum_mask(n, chunk):
    ii = lax.broadcasted_iota(jnp.int32, (n, n), 0)
    jj = lax.broadcasted_iota(jnp.int32, (n, n), 1)
    same = (ii // chunk) == (jj // chunk)
    return jnp.where(same & (ii >= jj), 1.0, 0.0).astype(BF16)


def _rmsnorm_kernel(x_ref, g_ref, o_ref):
    x = x_ref[...]
    ms = jnp.mean(x * x, axis=-1, keepdims=True)
    o_ref[...] = (x * lax.rsqrt(ms + NORM_EPS) * g_ref[...]).astype(o_ref.dtype)


def _rmsnorm(x2d, gain, out_dtype, tm=256):
    m, d = x2d.shape
    tm = min(tm, m)
    return pl.pallas_call(
        _rmsnorm_kernel,
        out_shape=jax.ShapeDtypeStruct((m, d), out_dtype),
        grid=(m // tm,),
        in_specs=[pl.BlockSpec((tm, d), lambda i: (i, 0)),
                  pl.BlockSpec((1, d), lambda i: (0, 0))],
        out_specs=pl.BlockSpec((tm, d), lambda i: (i, 0)),
        compiler_params=_params(("parallel",)),
        name="rmsnorm",
    )(x2d, gain.reshape(1, d).astype(F32))


def _matmul_kernel(*refs, nk, n_pairs, act, has_res):
    ab_refs = refs[:2 * n_pairs]
    res_ref = refs[2 * n_pairs] if has_res else None
    o_ref = refs[2 * n_pairs + int(has_res)]

    def product():
        acc = None
        for p in range(n_pairs):
            term = jnp.dot(ab_refs[2 * p][...], ab_refs[2 * p + 1][...].astype(BF16), preferred_element_type=F32)
            acc = term if acc is None else acc + term
        return acc

    if nk == 1:
        acc = product()
        if act == "relu2":
            r = jnp.maximum(acc, 0.0)
            acc = r * r
        elif act == "silu":
            acc = _silu(acc)
        elif isinstance(act, float):
            acc = acc * act
        if has_res:
            acc = res_ref[...] + acc
        o_ref[...] = acc.astype(o_ref.dtype)
    else:
        @pl.when(pl.program_id(2) == 0)
        def _first():
            o_ref[...] = res_ref[...] if has_res else jnp.zeros_like(o_ref)

        o_ref[...] += product()


def _matmul(a, b, *, layer=None, cols=None, a2=None, res=None, act=None, out_dtype=F32, tm=1024, tn=1024,
            tk=None, vmem_mib=56):
    m, ka = a.shape
    col0, n = (0, b.shape[-1]) if cols is None else cols
    n_pairs = 1 if a2 is None else 2
    tm, tn = min(tm, m), min(tn, n)
    tk = ka if tk is None else min(tk, ka)
    assert m % tm == 0 and n % tn == 0 and ka % tk == 0 and col0 % tn == 0
    jb0 = col0 // tn
    nk = ka // tk
    assert nk == 1 or (act is None and out_dtype == F32 and a2 is None)
    has_res = res is not None
    stacked = layer is not None

    def a_map(i, j, *k):
        return (i, k[0] if k else 0)

    def o_map(i, j, *k):
        return (i, j)

    def b_spec(row_block):
        def b_map(i, j, *k):
            kb = (k[0] if k else 0) + row_block
            return (layer, kb, j + jb0) if stacked else (kb, j + jb0)
        return pl.BlockSpec((None, tk, tn) if stacked else (tk, tn), b_map)

    a_spec = pl.BlockSpec((tm, tk), a_map)
    o_spec = pl.BlockSpec((tm, tn), o_map)
    in_specs = [a_spec, b_spec(0)]
    args = [a, b]
    if a2 is not None:
        in_specs += [a_spec, b_spec(1)]
        args += [a2, b]
    if has_res:
        in_specs.append(o_spec)
        args.append(res)
    grid = (m // tm, n // tn) + ((nk,) if nk > 1 else ())
    sem = ("parallel", "parallel") + (("arbitrary",) if nk > 1 else ())
    return pl.pallas_call(
        functools.partial(_matmul_kernel, nk=nk, n_pairs=n_pairs, act=act, has_res=has_res),
        out_shape=jax.ShapeDtypeStruct((m, n), out_dtype),
        grid=grid,
        in_specs=in_specs,
        out_specs=o_spec,
        compiler_params=_params(sem, vmem_mib),
        name="matmul",
    )(*args)


def _conv_proj_kernel(a_ref, b_ref, cw_ref, o_ref, carry_ref, raw_ref, *, tiles_per_seq, norm_dim, scale):
    i, j = pl.program_id(0), pl.program_id(1)
    tm, tn = o_ref.shape
    rows_per_dot = min(CONV_PROJ_ROWS, tm)
    first_tile = lax.rem(i, tiles_per_seq) == 0

    def epilogue(acc, r0, c0):
        cols = slice(c0, c0 + MXU_COLS)
        if r0 == 0:
            prev = carry_ref[j, :, cols]
            prev = jnp.where(first_tile, jnp.zeros_like(prev), prev)
        else:
            prev = raw_ref[r0 - SUBLANES:r0, cols]
        if r0 + rows_per_dot == tm:
            carry_ref[j, :, cols] = acc[rows_per_dot - SUBLANES:]
        ext = jnp.concatenate([prev, acc], axis=0)
        w = cw_ref[:, cols]
        y = w[CONV_K - 1:CONV_K] * acc
        for tap in range(CONV_K - 1):
            off = SUBLANES - (CONV_K - 1) + tap
            y = y + w[tap:tap + 1] * ext[off:off + rows_per_dot]
        y = _silu(y)
        if norm_dim is not None:
            segs = []
            for s0 in range(0, MXU_COLS, norm_dim):
                seg = y[:, s0:s0 + norm_dim]
                segs.append(seg * (lax.rsqrt(jnp.sum(seg * seg, axis=-1, keepdims=True) + NORM_EPS) * scale))
            y = jnp.concatenate(segs, axis=1)
        o_ref[r0:r0 + rows_per_dot, cols] = y.astype(o_ref.dtype)

    subtiles = [(r0, c0) for r0 in range(0, tm, rows_per_dot) for c0 in range(0, tn, MXU_COLS)]
    pending = None
    for r0, c0 in subtiles:
        raw_ref[r0:r0 + rows_per_dot, c0:c0 + MXU_COLS] = jnp.dot(
            a_ref[r0:r0 + rows_per_dot, :], b_ref[:, c0:c0 + MXU_COLS].astype(BF16), preferred_element_type=F32)
        if pending is not None:
            pr, pc = pending
            epilogue(raw_ref[pr:pr + rows_per_dot, pc:pc + MXU_COLS], pr, pc)
        pending = (r0, c0)
    pr, pc = pending
    epilogue(raw_ref[pr:pr + rows_per_dot, pc:pc + MXU_COLS], pr, pc)


def _conv_proj(a, b, conv_w, cols, seq_len, *, norm_dim=None, scale=1.0, tm=1024, tn=1024, vmem_mib=56):
    m, kdim = a.shape
    col0, n = cols
    tm, tn = min(tm, seq_len), min(tn, n)
    assert m % tm == 0 and n % tn == 0 and seq_len % tm == 0 and col0 % tn == 0
    jb0 = col0 // tn
    return pl.pallas_call(
        functools.partial(_conv_proj_kernel, tiles_per_seq=seq_len // tm, norm_dim=norm_dim, scale=scale),
        out_shape=jax.ShapeDtypeStruct((m, n), F32),
        grid=(m // tm, n // tn),
        in_specs=[pl.BlockSpec((tm, kdim), lambda i, j: (i, 0)),
                  pl.BlockSpec((kdim, tn), lambda i, j: (0, j + jb0)),
                  pl.BlockSpec((CONV_K, tn), lambda i, j: (0, j + jb0))],
        out_specs=pl.BlockSpec((tm, tn), lambda i, j: (i, j)),
        scratch_shapes=[pltpu.VMEM((n // tn, SUBLANES, tn), F32), pltpu.VMEM((tm, tn), F32)],
        compiler_params=_params(("arbitrary", "arbitrary"), vmem_mib),
        name="conv_proj",
    )(a, b, conv_w)


def _rope_proj_kernel(a_ref, b_ref, cos_ref, sin_ref, o_ref, *, head_dim, scale):
    acc = jnp.dot(a_ref[...], b_ref[...], preferred_element_type=F32)
    tm, tn = acc.shape
    even = (lax.broadcasted_iota(jnp.int32, (tm, LANES), 1) % 2) == 0
    parts = []
    for c0 in range(0, tn, LANES):
        x = acc[:, c0:c0 + LANES]
        partner = jnp.where(even, pltpu.roll(x, LANES - 1, axis=1), pltpu.roll(x, 1, axis=1))
        t0 = c0 % head_dim
        parts.append((x * cos_ref[:, t0:t0 + LANES] + partner * sin_ref[:, t0:t0 + LANES]) * scale)
    o_ref[...] = jnp.concatenate(parts, axis=1).astype(o_ref.dtype)


def _rope_proj(a, b, cos, sin, seq_len, head_dim, *, scale=1.0, tm=1024, tn=1024, vmem_mib=56):
    m, kdim = a.shape
    n = b.shape[1]
    tm, tn = min(tm, seq_len), min(tn, n)
    assert m % tm == 0 and n % tn == 0 and seq_len % tm == 0 and tn % head_dim == 0
    tiles_per_seq = seq_len // tm
    tab = pl.BlockSpec((tm, head_dim), lambda i, j: (i % tiles_per_seq, 0))
    return pl.pallas_call(
        functools.partial(_rope_proj_kernel, head_dim=head_dim, scale=scale),
        out_shape=jax.ShapeDtypeStruct((m, n), F32),
        grid=(m // tm, n // tn),
        in_specs=[pl.BlockSpec((tm, kdim), lambda i, j: (i, 0)),
                  pl.BlockSpec((kdim, tn), lambda i, j: (0, j)),
                  tab, tab],
        out_specs=pl.BlockSpec((tm, tn), lambda i, j: (i, j)),
        compiler_params=_params(("parallel", "parallel"), vmem_mib),
        name="rope_proj",
    )(a, b, cos, sin)


def _rope_kernel(inv_ref, sign_ref, cos_ref, sin_ref, *, tt):
    pos = (lax.broadcasted_iota(jnp.int32, cos_ref.shape, 0) + pl.program_id(0) * tt).astype(F32)
    ang = pos * inv_ref[...]
    cos_ref[...] = jnp.cos(ang)
    sin_ref[...] = jnp.sin(ang) * sign_ref[...]


def _rope_tables(t, head_dim, tt=512):
    tt = min(tt, t)
    half = head_dim // 2
    inv_freq = jnp.power(ROPE_BASE, -jnp.linspace(0.0, 1.0, half, dtype=F32))
    inv_pair = jnp.repeat(inv_freq, 2).reshape(1, head_dim)
    sign = jnp.tile(jnp.array([-1.0, 1.0], F32), half).reshape(1, head_dim)
    row = pl.BlockSpec((1, head_dim), lambda i: (0, 0))
    tab = pl.BlockSpec((tt, head_dim), lambda i: (i, 0))
    return pl.pallas_call(
        functools.partial(_rope_kernel, tt=tt),
        out_shape=(jax.ShapeDtypeStruct((t, head_dim), F32), jax.ShapeDtypeStruct((t, head_dim), F32)),
        grid=(t // tt,),
        in_specs=[row, row],
        out_specs=(tab, tab),
        compiler_params=_params(("parallel",)),
        name="rope_tables",
    )(inv_pair, sign)


def _gates_kernel(x_ref, alog_ref, dtb_ref, o_ref, *, n_heads, chunk):
    x = x_ref[...]
    beta = _sigmoid(x)
    g = -jnp.exp(alog_ref[...]) * _softplus(x + dtb_ref[...])
    gc = _split_dot(_chunk_cumsum_mask(x.shape[0], chunk), g, 3)
    lane = lax.broadcasted_iota(jnp.int32, x.shape, 1)
    o_ref[...] = jnp.where(lane < n_heads, beta, gc)


def _gates(ba, a_log, dt_bias, n_heads, chunk, tg=256):
    m, w = ba.shape
    tg = min(tg, m)
    alog_p = jnp.zeros((1, w), F32).at[0, n_heads:2 * n_heads].set(a_log.astype(F32))
    dtb_p = jnp.zeros((1, w), F32).at[0, n_heads:2 * n_heads].set(dt_bias.astype(F32))
    return pl.pallas_call(
        functools.partial(_gates_kernel, n_heads=n_heads, chunk=chunk),
        out_shape=jax.ShapeDtypeStruct((m, w), F32),
        grid=(m // tg,),
        in_specs=[pl.BlockSpec((tg, w), lambda i: (i, 0)),
                  pl.BlockSpec((1, w), lambda i: (0, 0)),
                  pl.BlockSpec((1, w), lambda i: (0, 0))],
        out_specs=pl.BlockSpec((tg, w), lambda i: (i, 0)),
        compiler_params=_params(("parallel",)),
        name="deltanet_gates",
    )(ba, alog_p, dtb_p)


def _unit_lower_inverses(mats, n):
    ii = lax.broadcasted_iota(jnp.int32, (n, n), 0)
    jj = lax.broadcasted_iota(jnp.int32, (n, n), 1)
    base = 16
    eye = jnp.where(ii == jj, 1.0, 0.0)
    diag_blocks = (ii // base) == (jj // base)
    ps = [jnp.where(diag_blocks, -a, 0.0) for a in mats]
    ts = [eye + p for p in ps]
    width = 2
    while width < base:
        ps = [_dot(p, p) for p in ps]
        ts = [t + _dot(t, p) for t, p in zip(ts, ps)]
        width *= 2
    bs = base
    while bs < n:
        off_blocks = ((ii // (2 * bs)) == (jj // (2 * bs))) & ((ii // bs) != (jj // bs))
        xs = [_dot(t, jnp.where(off_blocks, a, 0.0)) for t, a in zip(ts, mats)]
        ts = [t - _dot(x, t) for t, x in zip(ts, xs)]
        bs *= 2
    return ts


def _mixer_a_kernel(q_ref, k_ref, v_ref, z_ref, gate_ref, gct_ref, na_ref, o_ref, s_ref, *, tb, n_heads):
    d = A_HEAD_DIM
    heads = range(n_heads)

    @pl.when(pl.program_id(1) == 0)
    def _init():
        s_ref[...] = jnp.zeros_like(s_ref)

    q_all, k_all, v_all, z_all = q_ref[0], k_ref[0], v_ref[0], z_ref[0]
    gt = gate_ref[0]
    lane = lax.broadcasted_iota(jnp.int32, gt.shape, 1)

    ii = lax.broadcasted_iota(jnp.int32, (tb, tb), 0)
    jj = lax.broadcasted_iota(jnp.int32, (tb, tb), 1)
    causal = ii >= jj
    strict = ii > jj
    gain = na_ref[...]

    def head_cols(x, h):
        return x[:, h * d:(h + 1) * d]

    q = [head_cols(q_all, h) for h in heads]
    k = [head_cols(k_all, h) for h in heads]
    beta =[jnp.sum(jnp.where(lane == h, gt, 0.0), axis=1, keepdims=True) for h in heads]
    gc = [jnp.sum(jnp.where(lane == h + n_heads, gt, 0.0), axis=1, keepdims=True) for h in heads]
    grow = gct_ref[0]
    decay = [jnp.where(causal, jnp.exp(jnp.where(causal, gc[h] - grow[h:h + 1], 0.0)), 0.0) for h in heads]
    kb = [k[h] * beta[h] for h in heads]
    eg = [jnp.exp(gc[h]) for h in heads]

    kq = [_dot_nt(jnp.concatenate([kb[h], q[h]], axis=0), k[h]) for h in heads]
    a = [jnp.where(strict, kq[h][:tb] * decay[h], 0.0) for h in heads]
    scores = [kq[h][tb:] * decay[h] for h in heads]
    t = _unit_lower_inverses(a, tb)
    uw = [_dot(t[h], jnp.concatenate([head_cols(v_all, h) * beta[h], kb[h] * eg[h]], axis=1)) for h in heads]
    s = [s_ref[h] for h in heads]
    ws = [_dot(jnp.concatenate([uw[h][:, d:], q[h] * eg[h]], axis=0), s[h]) for h in heads]
    v_new = [uw[h][:, :d] - ws[h][:tb] for h in heads]
    o = [ws[h][tb:] + _dot(scores[h], v_new[h]) for h in heads]
    g_last = [gc[h][tb - 1:tb] for h in heads]
    kv = [_dot_tn(k[h] * jnp.exp(g_last[h] - gc[h]), v_new[h]) for h in heads]
    for h in heads:
        s_ref[h] = s[h] * jnp.exp(g_last[h]) + kv[h]
        on = o[h] * lax.rsqrt(jnp.mean(o[h] * o[h], axis=-1, keepdims=True) + NORM_EPS) * gain
        o_ref[0, :, h * d:(h + 1) * d] = (on * head_cols(z_all, h)).astype(o_ref.dtype)


def _mixer_a(q, k, v, z, gates, gates_t, norm_a, n_heads):
    b, t, hd = q.shape
    d = A_HEAD_DIM
    tb = min(A_CHUNK, t)
    hh = n_heads
    col = pl.BlockSpec((1, tb, hd), lambda bi, ti: (bi, ti, 0))
    return pl.pallas_call(
        functools.partial(_mixer_a_kernel, tb=tb, n_heads=hh),
        out_shape=jax.ShapeDtypeStruct((b, t, hd), BF16),
        grid=(b, t // tb),
        in_specs=[col, col, col, col,
                  pl.BlockSpec((1, tb, gates.shape[-1]), lambda bi, ti: (bi, ti, 0)),
                  pl.BlockSpec((1, hh, tb), lambda bi, ti: (bi, 0, ti)),
                  pl.BlockSpec((1, d), lambda bi, ti: (0, 0))],
        out_specs=col,
        scratch_shapes=[pltpu.VMEM((hh, d, d), F32)],
        compiler_params=_params(("parallel", "arbitrary")),
        name="mixer_deltanet",
    )(q, k, v, z, gates, gates_t, norm_a.reshape(1, d).astype(F32))


def _mixer_b_kernel(q_ref, k_ref, v_ref, g_ref, nb_ref, o_ref, s_ref, decay_ref, *, chunk, n_heads):
    d = B_HEAD_DIM
    heads = range(n_heads)
    log_gamma = [math.log1p(-(2.0 ** (-5.0 - h))) for h in heads]

    @pl.when(pl.program_id(1) == 0)
    def _init():
        s_ref[...] = jnp.zeros_like(s_ref)
        ii = lax.broadcasted_iota(jnp.int32, (chunk, chunk), 0)
        jj = lax.broadcasted_iota(jnp.int32, (chunk, chunk), 1)
        causal = ii >= jj
        dist = jnp.where(causal, ii - jj, 0).astype(F32)
        for h in heads:
            decay_ref[h] = jnp.where(causal, jnp.exp(dist * log_gamma[h]), 0.0)

    pos = lax.broadcasted_iota(jnp.int32, (chunk, 1), 0).astype(F32)
    q_all, k_all, v_all, g_all = q_ref[0], k_ref[0], v_ref[0], g_ref[0]

    def head_cols(x, h):
        return x[:, h * d:(h + 1) * d]

    q = [head_cols(q_all, h) for h in heads]
    k = [head_cols(k_all, h) for h in heads]
    v = [head_cols(v_all, h) for h in heads]
    s = [s_ref[h] for h in heads]
    scores = [_dot_nt(q[h], k[h]) * decay_ref[h] for h in heads]
    cross = [_dot(q[h] * jnp.exp((pos + 1.0) * log_gamma[h]), s[h]) for h in heads]
    o = [_dot(scores[h], v[h]) + cross[h] for h in heads]
    kv = [_dot_tn(k[h] * jnp.exp((chunk - 1.0 - pos) * log_gamma[h]), v[h]) for h in heads]
    gain = nb_ref[...]
    for h in heads:
        s_ref[h] = s[h] * math.exp(chunk * log_gamma[h]) + kv[h]
        on = o[h] * lax.rsqrt(jnp.mean(o[h] * o[h], axis=-1, keepdims=True) + NORM_EPS) * gain
        o_ref[0, :, h * d:(h + 1) * d] = (on * head_cols(g_all, h)).astype(o_ref.dtype)


def _mixer_b(q, k, v, g, norm_b, n_heads):
    b, t, hd = q.shape
    d = B_HEAD_DIM
    chunk = min(B_CHUNK, t)
    hh = n_heads
    col = pl.BlockSpec((1, chunk, hd), lambda bi, ti: (bi, ti, 0))
    return pl.pallas_call(
        functools.partial(_mixer_b_kernel, chunk=chunk, n_heads=hh),
        out_shape=jax.ShapeDtypeStruct((b, t, hd), BF16),
        grid=(b, t // chunk),
        in_specs=[col, col, col, col, pl.BlockSpec((1, d), lambda bi, ti: (0, 0))],
        out_specs=col,
        scratch_shapes=[pltpu.VMEM((hh, d, d), F32), pltpu.VMEM((hh, chunk, chunk), F32)],
        compiler_params=_params(("arbitrary", "arbitrary")),
        name="mixer_retention",
    )(q, k, v, g, norm_b.reshape(1, d).astype(F32))


def _mixer_c_kernel(q_ref, k_ref, v_ref, r_ref, lr_ref, wup_ref, bgk_ref, nc_ref, o_ref, st_ref, *, chunk, n_heads):
    heads = range(n_heads)

    @pl.when(pl.program_id(1) == 0)
    def _init():
        st_ref[...] = jnp.zeros_like(st_ref)

    dk = q_ref.shape[-1] // n_heads
    dv = v_ref.shape[-1] // n_heads
    q_all, k_all, v_all, r_all = q_ref[0], k_ref[0], v_ref[0], r_ref[0]

    logit = _dot(lr_ref[0], wup_ref[...]) + bgk_ref[...]
    gk = -_softplus(-logit) / GK_NORMALIZER
    gc_all = _split_dot(_chunk_cumsum_mask(chunk, chunk), gk, 2)
    sub = min(C_SUBCHUNK, chunk)
    ii = lax.broadcasted_iota(jnp.int32, (sub, sub), 0)
    jj = lax.broadcasted_iota(jnp.int32, (sub, sub), 1)
    causal = ii >= jj

    q = [q_all[:, h * dk:(h + 1) * dk] for h in heads]
    k = [k_all[:, h * dk:(h + 1) * dk] for h in heads]
    v = [v_all[:, h * dv:(h + 1) * dv] for h in heads]
    gc = [gc_all[:, h * dk:(h + 1) * dk] for h in heads]
    g_last = [gc[h][chunk - 1:chunk] for h in heads]
    st = [st_ref[h] for h in heads]

    def score_rows(h, r0):
        rows = slice(r0, r0 + sub)
        mid = gc[h][r0 + sub // 2 - 1:r0 + sub // 2]
        diag = jnp.where(causal, _dot_nt(q[h][rows] * jnp.exp(gc[h][rows] - mid),
                                         k[h][rows] * jnp.exp(mid - gc[h][rows])), 0.0)
        parts = [diag]
        if r0 > 0:
            bnd = gc[h][r0 - 1:r0]
            parts.insert(0, _dot_nt(q[h][rows] * jnp.exp(gc[h][rows] - bnd), k[h][:r0] * jnp.exp(bnd - gc[h][:r0])))
        if r0 + sub < chunk:
            parts.append(jnp.zeros((sub, chunk - r0 - sub), F32))
        return jnp.concatenate(parts, axis=1) if len(parts) > 1 else diag

    score_blocks = [[score_rows(h, r0) for h in heads] for r0 in range(0, chunk, sub)]
    scores = [jnp.concatenate([blk[h] for blk in score_blocks], axis=0) if len(score_blocks) > 1
              else score_blocks[0][h] for h in heads]
    cross = [_dot_nt(q[h] * jnp.exp(gc[h]), st[h]) for h in heads]
    o = [_dot(scores[h], v[h]) + cross[h] for h in heads]
    kv = [_dot_tn(v[h], k[h] * jnp.exp(g_last[h] - gc[h])) for h in heads]
    gain = nc_ref[...]
    for h in heads:
        st_ref[h] = st[h] * jnp.exp(g_last[h]) + kv[h]
        on = o[h] * lax.rsqrt(jnp.mean(o[h] * o[h], axis=-1, keepdims=True) + NORM_EPS) * gain
        o_ref[0, :, h * dv:(h + 1) * dv] = (on * r_all[:, h * dv:(h + 1) * dv]).astype(o_ref.dtype)


def _mixer_c(q, k, v, r, lr, w_up, b_gk, norm_c):
    b, t, kw = q.shape
    vw = v.shape[-1]
    hh = C_HEADS
    dk, dv = kw // hh, vw // hh
    chunk = min(C_CHUNK, t)
    rw = lr.shape[-1]
    kcol = pl.BlockSpec((1, chunk, kw), lambda bi, ti: (bi, ti, 0))
    vcol = pl.BlockSpec((1, chunk, vw), lambda bi, ti: (bi, ti, 0))
    return pl.pallas_call(
        functools.partial(_mixer_c_kernel, chunk=chunk, n_heads=hh),
        out_shape=jax.ShapeDtypeStruct((b, t, vw), BF16),
        grid=(b, t // chunk),
        in_specs=[kcol, kcol, vcol, vcol,
                  pl.BlockSpec((1, chunk, rw), lambda bi, ti: (bi, ti, 0)),
                  pl.BlockSpec((rw, kw), lambda bi, ti: (0, 0)),
                  pl.BlockSpec((1, kw), lambda bi, ti: (0, 0)),
                  pl.BlockSpec((1, dv), lambda bi, ti: (0, 0))],
        out_specs=vcol,
        scratch_shapes=[pltpu.VMEM((hh, dv, dk), F32)],
        compiler_params=_params(("parallel", "arbitrary")),
        name="mixer_gla",
    )(q, k, v, r, lr, w_up, b_gk.reshape(1, kw).astype(F32), norm_c.reshape(1, dv).astype(F32))


def _pad_cols(w, width):
    return jnp.pad(w, ((0, 0), (0, width - w.shape[1])))


def _mlp(x2d, gain, w_up, w_down, layer):
    hn = _rmsnorm(x2d, gain, BF16)
    hid = _matmul(hn, w_up, layer=layer, act="relu2", out_dtype=BF16, tn=512)
    return _matmul(hid, w_down, layer=layer, res=x2d, tk=4096)


def _even_layer(x2d, b, t, gain, w_in, conv_w, a_log, dt_bias, norm_a, norm_b, w_out):
    d = x2d.shape[1]
    ha, hb = d // 256, d // 512
    akw = ha * A_HEAD_DIM
    bkw = hb * B_HEAD_DIM
    small0 = 4 * akw
    b0 = small0 + 2 * ha
    conv_w = conv_w.astype(F32)

    def seq(a2d):
        return a2d.reshape(b, t, -1)

    hn = _rmsnorm(x2d, gain, BF16)
    narrow = dict(tn=512)

    def w_slice(lo, width):
        return w_in[:, lo:lo + width].astype(BF16)

    q_a = seq(_conv_proj(hn, w_in, conv_w, (0, akw), t, norm_dim=A_HEAD_DIM, scale=A_HEAD_DIM ** -0.5, **narrow))
    k_a = seq(_conv_proj(hn, w_in, conv_w, (akw, akw), t, norm_dim=A_HEAD_DIM, **narrow))
    v_a = seq(_conv_proj(hn, w_in, conv_w, (2 * akw, akw), t, **narrow))
    z_a = seq(_matmul(hn, w_in, cols=(3 * akw, akw), act="silu", **narrow))
    ba = _matmul(hn, _pad_cols(w_slice(small0, 2 * ha), LANES))
    gates = _gates(ba, a_log, dt_bias, ha, A_CHUNK).reshape(b, t, LANES)
    gates_t = jnp.swapaxes(gates[:, :, ha:2 * ha], 1, 2)
    o_a = _mixer_a(q_a, k_a, v_a, z_a, gates, gates_t, norm_a, ha)
    cos, sin = _rope_tables(t, B_HEAD_DIM)
    q_b = seq(_rope_proj(hn, w_slice(b0, bkw), cos, sin, t, B_HEAD_DIM))
    k_b = seq(_rope_proj(hn, w_slice(b0 + bkw, bkw), cos, sin, t, B_HEAD_DIM, scale=B_HEAD_DIM ** -0.5))
    v_b = seq(_matmul(hn, w_slice(b0 + 2 * bkw, bkw)))
    g_b = seq(_matmul(hn, w_slice(b0 + 3 * bkw, bkw), act="silu"))
    o_b = _mixer_b(q_b, k_b, v_b, g_b, norm_b, hb)
    return _matmul(o_a.reshape(b * t, akw), w_out.astype(BF16), a2=o_b.reshape(b * t, bkw), res=x2d)


def _odd_layer(x2d, b, t, gain, w_in, w_gk_down, w_gk_up, b_gk, norm_c, w_out):
    d = x2d.shape[1]
    kw, vw = d // 2, d
    dk = kw // C_HEADS

    def seq(a2d):
        return a2d.reshape(b, t, -1)

    hn = _rmsnorm(x2d, gain, BF16)
    w_bf = w_in.astype(BF16)
    q = seq(_matmul(hn, w_bf, cols=(0, kw), act=dk ** -0.5))
    k = seq(_matmul(hn, w_bf, cols=(kw, kw)))
    v = seq(_matmul(hn, w_bf, cols=(2 * kw, vw)))
    r = seq(_matmul(hn, w_bf, cols=(2 * kw + vw, vw), act="silu"))
    lr = seq(_matmul(hn, _pad_cols(w_gk_down, LANES).astype(BF16)))
    w_up = jnp.pad(w_gk_up, ((0, LANES - w_gk_up.shape[0]), (0, 0))).astype(BF16)
    o_c = _mixer_c(q, k, v, r, lr, w_up, b_gk, norm_c).reshape(b * t, vw)
    return _matmul(o_c, w_out.astype(BF16), res=x2d)


def kernel(x, norm_mix, norm_mlp, norm_final, w_up, w_down, w_in_ab, conv_a, a_log, dt_bias, norm_a, norm_b,
           w_out_ab, w_in_c, w_gk_down, w_gk_up, b_gk, norm_c, w_out_c):
    b, t, d = x.shape
    depth = norm_mix.shape[0]
    x2d = x.reshape(b * t, d)
    w_down = w_down.astype(BF16)
    for layer in range(depth):
        i = layer // 2
        if layer % 2 == 0:
            x2d = _even_layer(x2d, b, t, norm_mix[layer], w_in_ab[i], conv_a[i], a_log[i], dt_bias[i],
                              norm_a[i], norm_b[i], w_out_ab[i])
        else:
            x2d = _odd_layer(x2d, b, t, norm_mix[layer], w_in_c[i], w_gk_down[i], w_gk_up[i], b_gk[i],
                             norm_c[i], w_out_c[i])
        x2d = _mlp(x2d, norm_mlp[layer], w_up, w_down, layer)
    return _rmsnorm(x2d, norm_final, F32).reshape(b, t, d)
```

```python
import functools
import math

import jax
import jax.numpy as jnp
from jax import lax
from jax.experimental import pallas as pl
from jax.experimental.pallas import tpu as pltpu

F32 = jnp.float32
BF16 = jnp.bfloat16

NORM_EPS = 1e-6
ROPE_BASE = 10000.0
CONV_K = 4
GK_RANK = 16
GK_NORMALIZER = 16.0
A_HEAD_DIM = 128
B_HEAD_DIM = 256
C_HEADS = 4
LANES = 128
SUBLANES = 8
MXU_COLS = 256
CONV_PROJ_ROWS = 512
A_CHUNK = 128
B_CHUNK = 256
C_CHUNK = 256
C_SUBCHUNK = 128
MIB = 1024 * 1024


def _params(semantics, vmem_mib=None):
    kwargs = dict(dimension_semantics=semantics)
    if vmem_mib is not None:
        kwargs["vmem_limit_bytes"] = vmem_mib * MIB
    return pltpu.CompilerParams(**kwargs)


def _dot(a, b):
    return jnp.dot(a.astype(BF16), b.astype(BF16), preferred_element_type=F32)


def _dot_nt(a, b):
    return lax.dot_general(a.astype(BF16), b.astype(BF16), (((1,), (1,)), ((), ())),
                           preferred_element_type=F32)


def _dot_tn(a, b):
    return lax.dot_general(a.astype(BF16), b.astype(BF16), (((0,), (0,)), ((), ())),
                           preferred_element_type=F32)


def _sigmoid(x):
    return 1.0 / (1.0 + jnp.exp(-x))


def _silu(x):
    return x * _sigmoid(x)


def _softplus(x):
    return jnp.maximum(x, 0.0) + jnp.log1p(jnp.exp(-jnp.abs(x)))


def _split_dot(mask_bf16, g, pieces):
    acc = None
    rem = g
    for _ in range(pieces):
        part = rem.astype(BF16)
        term = jnp.dot(mask_bf16, part, preferred_element_type=F32)
        acc = term if acc is None else acc + term
        rem = rem - part.astype(F32)
    return acc


def _chunk_cumsum_mask(n, chunk):
    ii = lax.broadcasted_iota(jnp.int32, (n, n), 0)
    jj = lax.broadcasted_iota(jnp.int32, (n, n), 1)
    same = (ii // chunk) == (jj // chunk)
    return jnp.where(same & (ii >= jj), 1.0, 0.0).astype(BF16)


def _rmsnorm_kernel(x_ref, g_ref, o_ref):
    x = x_ref[...]
    ms = jnp.mean(x * x, axis=-1, keepdims=True)
    o_ref[...] = (x * lax.rsqrt(ms + NORM_EPS) * g_ref[...]).astype(o_ref.dtype)


def _rmsnorm(x2d, gain, out_dtype, tm=512):
    m, d = x2d.shape
    tm = min(tm, m)
    return pl.pallas_call(
        _rmsnorm_kernel,
        out_shape=jax.ShapeDtypeStruct((m, d), out_dtype),
        grid=(m // tm,),
        in_specs=[pl.BlockSpec((tm, d), lambda i: (i, 0)),
                  pl.BlockSpec((1, d), lambda i: (0, 0))],
        out_specs=pl.BlockSpec((tm, d), lambda i: (i, 0)),
        compiler_params=_params(("parallel",)),
        name="rmsnorm",
    )(x2d, gain.reshape(1, d).astype(F32))


def _matmul_kernel(*refs, nk, n_pairs, act, has_res):
    ab_refs = refs[:2 * n_pairs]
    res_ref = refs[2 * n_pairs] if has_res else None
    o_ref = refs[2 * n_pairs + int(has_res)]

    def product():
        acc = None
        for p in range(n_pairs):
            term = jnp.dot(ab_refs[2 * p][...], ab_refs[2 * p + 1][...], preferred_element_type=F32)
            acc = term if acc is None else acc + term
        return acc

    if nk == 1:
        acc = product()
        if act == "relu2":
            r = jnp.maximum(acc, 0.0)
            acc = r * r
        elif act == "silu":
            acc = _silu(acc)
        elif isinstance(act, float):
            acc = acc * act
        if has_res:
            acc = res_ref[...] + acc
        o_ref[...] = acc.astype(o_ref.dtype)
    else:
        @pl.when(pl.program_id(2) == 0)
        def _first():
            o_ref[...] = res_ref[...] if has_res else jnp.zeros_like(o_ref)

        o_ref[...] += product()


def _matmul(a, b, *, layer=None, cols=None, a2=None, res=None, act=None, out_dtype=F32, tm=1024, tn=1024,
            tk=None, vmem_mib=56):
    m, ka = a.shape
    col0, n = (0, b.shape[-1]) if cols is None else cols
    n_pairs = 1 if a2 is None else 2
    tm, tn = min(tm, m), min(tn, n)
    tk = ka if tk is None else min(tk, ka)
    assert m % tm == 0 and n % tn == 0 and ka % tk == 0 and col0 % tn == 0
    jb0 = col0 // tn
    nk = ka // tk
    assert nk == 1 or (act is None and out_dtype == F32 and a2 is None)
    has_res = res is not None
    stacked = layer is not None

    def a_map(i, j, *k):
        return (i, k[0] if k else 0)

    def o_map(i, j, *k):
        return (i, j)

    def b_spec(row_block):
        def b_map(i, j, *k):
            kb = (k[0] if k else 0) + row_block
            return (layer, kb, j + jb0) if stacked else (kb, j + jb0)
        return pl.BlockSpec((None, tk, tn) if stacked else (tk, tn), b_map)

    a_spec = pl.BlockSpec((tm, tk), a_map)
    o_spec = pl.BlockSpec((tm, tn), o_map)
    in_specs = [a_spec, b_spec(0)]
    args = [a, b]
    if a2 is not None:
        in_specs += [a_spec, b_spec(1)]
        args += [a2, b]
    if has_res:
        in_specs.append(o_spec)
        args.append(res)
    grid = (m // tm, n // tn) + ((nk,) if nk > 1 else ())
    sem = ("parallel", "parallel") + (("arbitrary",) if nk > 1 else ())
    return pl.pallas_call(
        functools.partial(_matmul_kernel, nk=nk, n_pairs=n_pairs, act=act, has_res=has_res),
        out_shape=jax.ShapeDtypeStruct((m, n), out_dtype),
        grid=grid,
        in_specs=in_specs,
        out_specs=o_spec,
        compiler_params=_params(sem, vmem_mib),
        name="matmul",
    )(*args)


def _conv_proj_kernel(a_ref, b_ref, cw_ref, o_ref, carry_ref, raw_ref, *, tiles_per_seq, norm_dim, scale):
    i, j = pl.program_id(0), pl.program_id(1)
    tm, tn = o_ref.shape
    rows_per_dot = min(CONV_PROJ_ROWS, tm)
    first_tile = lax.rem(i, tiles_per_seq) == 0

    prev = carry_ref[j]
    raw_ref[0:SUBLANES, :] = jnp.where(first_tile, jnp.zeros_like(prev), prev)

    def epilogue(r0, c0):
        cols = slice(c0, c0 + MXU_COLS)
        if r0 + rows_per_dot == tm:
            carry_ref[j, :, cols] = raw_ref[tm:tm + SUBLANES, cols]
        w = cw_ref[:, cols]
        y = w[CONV_K - 1:CONV_K] * raw_ref[SUBLANES + r0:SUBLANES + r0 + rows_per_dot, cols]
        for tap in range(CONV_K - 1):
            off = SUBLANES - (CONV_K - 1) + tap + r0
            y = y + w[tap:tap + 1] * raw_ref[off:off + rows_per_dot, cols]
        y = _silu(y)
        if norm_dim is not None:
            segs = []
            for s0 in range(0, MXU_COLS, norm_dim):
                seg = y[:, s0:s0 + norm_dim]
                segs.append(seg * (lax.rsqrt(jnp.sum(seg * seg, axis=-1, keepdims=True) + NORM_EPS) * scale))
            y = jnp.concatenate(segs, axis=1)
        o_ref[r0:r0 + rows_per_dot, cols] = y.astype(o_ref.dtype)

    subtiles = [(r0, c0) for r0 in range(0, tm, rows_per_dot) for c0 in range(0, tn, MXU_COLS)]
    pending = None
    for r0, c0 in subtiles:
        raw_ref[SUBLANES + r0:SUBLANES + r0 + rows_per_dot, c0:c0 + MXU_COLS] = jnp.dot(
            a_ref[r0:r0 + rows_per_dot, :], b_ref[:, c0:c0 + MXU_COLS], preferred_element_type=F32)
        if pending is not None:
            epilogue(*pending)
        pending = (r0, c0)
    epilogue(*pending)


def _conv_proj(a, b, conv_w, cols, seq_len, *, norm_dim=None, scale=1.0, tm=1024, tn=1024, vmem_mib=56):
    m, kdim = a.shape
    col0, n = cols
    tm, tn = min(tm, seq_len), min(tn, n)
    assert m % tm == 0 and n % tn == 0 and seq_len % tm == 0 and col0 % tn == 0
    jb0 = col0 // tn
    return pl.pallas_call(
        functools.partial(_conv_proj_kernel, tiles_per_seq=seq_len // tm, norm_dim=norm_dim, scale=scale),
        out_shape=jax.ShapeDtypeStruct((m, n), F32),
        grid=(m // tm, n // tn),
        in_specs=[pl.BlockSpec((tm, kdim), lambda i, j: (i, 0)),
                  pl.BlockSpec((kdim, tn), lambda i, j: (0, j + jb0)),
                  pl.BlockSpec((CONV_K, tn), lambda i, j: (0, j + jb0))],
        out_specs=pl.BlockSpec((tm, tn), lambda i, j: (i, j)),
        scratch_shapes=[pltpu.VMEM((n // tn, SUBLANES, tn), F32), pltpu.VMEM((tm + SUBLANES, tn), F32)],
        compiler_params=_params(("arbitrary", "arbitrary"), vmem_mib),
        name="conv_proj",
    )(a, b, conv_w)


def _rope_proj_kernel(a_ref, b_ref, cos_ref, sin_ref, o_ref, *, head_dim, scale):
    acc = jnp.dot(a_ref[...], b_ref[...], preferred_element_type=F32)
    tm, tn = acc.shape
    even = (lax.broadcasted_iota(jnp.int32, (tm, LANES), 1) % 2) == 0
    parts = []
    for c0 in range(0, tn, LANES):
        x = acc[:, c0:c0 + LANES]
        partner = jnp.where(even, pltpu.roll(x, LANES - 1, axis=1), pltpu.roll(x, 1, axis=1))
        t0 = c0 % head_dim
        parts.append((x * cos_ref[:, t0:t0 + LANES] + partner * sin_ref[:, t0:t0 + LANES]) * scale)
    o_ref[...] = jnp.concatenate(parts, axis=1).astype(o_ref.dtype)


def _rope_proj(a, b, cos, sin, seq_len, head_dim, *, scale=1.0, tm=1024, tn=1024, vmem_mib=56):
    m, kdim = a.shape
    n = b.shape[1]
    tm, tn = min(tm, seq_len), min(tn, n)
    assert m % tm == 0 and n % tn == 0 and seq_len % tm == 0 and tn % head_dim == 0
    tiles_per_seq = seq_len // tm
    tab = pl.BlockSpec((tm, head_dim), lambda i, j: (i % tiles_per_seq, 0))
    return pl.pallas_call(
        functools.partial(_rope_proj_kernel, head_dim=head_dim, scale=scale),
        out_shape=jax.ShapeDtypeStruct((m, n), F32),
        grid=(m // tm, n // tn),
        in_specs=[pl.BlockSpec((tm, kdim), lambda i, j: (i, 0)),
                  pl.BlockSpec((kdim, tn), lambda i, j: (0, j)),
                  tab, tab],
        out_specs=pl.BlockSpec((tm, tn), lambda i, j: (i, j)),
        compiler_params=_params(("parallel", "parallel"), vmem_mib),
        name="rope_proj",
    )(a, b, cos, sin)


def _rope_kernel(inv_ref, sign_ref, cos_ref, sin_ref, *, tt):
    pos = (lax.broadcasted_iota(jnp.int32, cos_ref.shape, 0) + pl.program_id(0) * tt).astype(F32)
    ang = pos * inv_ref[...]
    cos_ref[...] = jnp.cos(ang)
    sin_ref[...] = jnp.sin(ang) * sign_ref[...]


def _rope_tables(t, head_dim, tt=512):
    tt = min(tt, t)
    half = head_dim // 2
    inv_freq = jnp.power(ROPE_BASE, -jnp.linspace(0.0, 1.0, half, dtype=F32))
    inv_pair = jnp.repeat(inv_freq, 2).reshape(1, head_dim)
    sign = jnp.tile(jnp.array([-1.0, 1.0], F32), half).reshape(1, head_dim)
    row = pl.BlockSpec((1, head_dim), lambda i: (0, 0))
    tab = pl.BlockSpec((tt, head_dim), lambda i: (i, 0))
    return pl.pallas_call(
        functools.partial(_rope_kernel, tt=tt),
        out_shape=(jax.ShapeDtypeStruct((t, head_dim), F32), jax.ShapeDtypeStruct((t, head_dim), F32)),
        grid=(t // tt,),
        in_specs=[row, row],
        out_specs=(tab, tab),
        compiler_params=_params(("parallel",)),
        name="rope_tables",
    )(inv_pair, sign)


def _gates_kernel(x_ref, alog_ref, dtb_ref, o_ref, *, n_heads, chunk):
    x = x_ref[...]
    beta = _sigmoid(x)
    g = -jnp.exp(alog_ref[...]) * _softplus(x + dtb_ref[...])
    gc = _split_dot(_chunk_cumsum_mask(x.shape[0], chunk), g, 3)
    lane = lax.broadcasted_iota(jnp.int32, x.shape, 1)
    o_ref[...] = jnp.where(lane < n_heads, beta, gc)


def _gates(ba, a_log, dt_bias, n_heads, chunk, tg=256):
    m, w = ba.shape
    tg = min(tg, m)
    alog_p = jnp.zeros((1, w), F32).at[0, n_heads:2 * n_heads].set(a_log.astype(F32))
    dtb_p = jnp.zeros((1, w), F32).at[0, n_heads:2 * n_heads].set(dt_bias.astype(F32))
    return pl.pallas_call(
        functools.partial(_gates_kernel, n_heads=n_heads, chunk=chunk),
        out_shape=jax.ShapeDtypeStruct((m, w), F32),
        grid=(m // tg,),
        in_specs=[pl.BlockSpec((tg, w), lambda i: (i, 0)),
                  pl.BlockSpec((1, w), lambda i: (0, 0)),
                  pl.BlockSpec((1, w), lambda i: (0, 0))],
        out_specs=pl.BlockSpec((tg, w), lambda i: (i, 0)),
        compiler_params=_params(("parallel",)),
        name="deltanet_gates",
    )(ba, alog_p, dtb_p)


def _unit_lower_inverses(mats, n):
    ii = lax.broadcasted_iota(jnp.int32, (n, n), 0)
    jj = lax.broadcasted_iota(jnp.int32, (n, n), 1)
    base = 16
    eye = jnp.where(ii == jj, 1.0, 0.0)
    diag_blocks = (ii // base) == (jj // base)
    ps = [jnp.where(diag_blocks, -a, 0.0) for a in mats]
    ts = [eye + p for p in ps]
    width = 2
    while width < base:
        ps = [_dot(p, p) for p in ps]
        ts = [t + _dot(t, p) for t, p in zip(ts, ps)]
        width *= 2
    bs = base
    while bs < n:
        off_blocks = ((ii // (2 * bs)) == (jj // (2 * bs))) & ((ii // bs) != (jj // bs))
        xs = [_dot(t, jnp.where(off_blocks, a, 0.0)) for t, a in zip(ts, mats)]
        ts = [t - _dot(x, t) for t, x in zip(ts, xs)]
        bs *= 2
    return ts


def _mixer_a_kernel(q_ref, k_ref, v_ref, z_ref, gate_ref, gct_ref, na_ref, o_ref, s_ref, *, tb, n_heads):
    d = A_HEAD_DIM
    heads = range(n_heads)

    @pl.when(pl.program_id(1) == 0)
    def _init():
        s_ref[...] = jnp.zeros_like(s_ref)

    q_all, k_all, v_all, z_all = q_ref[0], k_ref[0], v_ref[0], z_ref[0]
    gt = gate_ref[0]
    lane = lax.broadcasted_iota(jnp.int32, gt.shape, 1)

    ii = lax.broadcasted_iota(jnp.int32, (tb, tb), 0)
    jj = lax.broadcasted_iota(jnp.int32, (tb, tb), 1)
    causal = ii >= jj
    strict = ii > jj
    gain = na_ref[...]

    def head_cols(x, h):
        return x[:, h * d:(h + 1) * d]

    q = [head_cols(q_all, h) for h in heads]
    k = [head_cols(k_all, h) for h in heads]
    beta =[jnp.sum(jnp.where(lane == h, gt, 0.0), axis=1, keepdims=True) for h in heads]
    gc = [jnp.sum(jnp.where(lane == h + n_heads, gt, 0.0), axis=1, keepdims=True) for h in heads]
    grow = gct_ref[0]
    decay = [jnp.where(causal, jnp.exp(jnp.where(causal, gc[h] - grow[h:h + 1], 0.0)), 0.0) for h in heads]
    kb = [k[h] * beta[h] for h in heads]
    eg = [jnp.exp(gc[h]) for h in heads]

    kq = [_dot_nt(jnp.concatenate([kb[h], q[h]], axis=0), k[h]) for h in heads]
    a = [jnp.where(strict, kq[h][:tb] * decay[h], 0.0) for h in heads]
    scores = [kq[h][tb:] * decay[h] for h in heads]
    t = _unit_lower_inverses(a, tb)
    uw = [_dot(t[h], jnp.concatenate([head_cols(v_all, h) * beta[h], kb[h] * eg[h]], axis=1)) for h in heads]
    s = [s_ref[h] for h in heads]
    ws = [_dot(jnp.concatenate([uw[h][:, d:], q[h] * eg[h]], axis=0), s[h]) for h in heads]
    v_new = [uw[h][:, :d] - ws[h][:tb] for h in heads]
    o = [ws[h][tb:] + _dot(scores[h], v_new[h]) for h in heads]
    g_last = [gc[h][tb - 1:tb] for h in heads]
    kv = [_dot_tn(k[h] * jnp.exp(g_last[h] - gc[h]), v_new[h]) for h in heads]
    for h in heads:
        s_ref[h] = s[h] * jnp.exp(g_last[h]) + kv[h]
        on = o[h] * lax.rsqrt(jnp.mean(o[h] * o[h], axis=-1, keepdims=True) + NORM_EPS) * gain
        o_ref[0, :, h * d:(h + 1) * d] = (on * head_cols(z_all, h)).astype(o_ref.dtype)


def _mixer_a(q, k, v, z, gates, gates_t, norm_a, n_heads):
    b, t, hd = q.shape
    d = A_HEAD_DIM
    tb = min(A_CHUNK, t)
    hh = n_heads
    col = pl.BlockSpec((1, tb, hd), lambda bi, ti: (bi, ti, 0))
    return pl.pallas_call(
        functools.partial(_mixer_a_kernel, tb=tb, n_heads=hh),
        out_shape=jax.ShapeDtypeStruct((b, t, hd), BF16),
        grid=(b, t // tb),
        in_specs=[col, col, col, col,
                  pl.BlockSpec((1, tb, gates.shape[-1]), lambda bi, ti: (bi, ti, 0)),
                  pl.BlockSpec((1, hh, tb), lambda bi, ti: (bi, 0, ti)),
                  pl.BlockSpec((1, d), lambda bi, ti: (0, 0))],
        out_specs=col,
        scratch_shapes=[pltpu.VMEM((hh, d, d), F32)],
        compiler_params=_params(("parallel", "arbitrary")),
        name="mixer_deltanet",
    )(q, k, v, z, gates, gates_t, norm_a.reshape(1, d).astype(F32))


def _mixer_b_kernel(q_ref, k_ref, v_ref, g_ref, nb_ref, o_ref, s_ref, decay_ref, *, chunk, n_heads):
    d = B_HEAD_DIM
    heads = range(n_heads)
    log_gamma = [math.log1p(-(2.0 ** (-5.0 - h))) for h in heads]

    @pl.when(pl.program_id(1) == 0)
    def _init():
        s_ref[...] = jnp.zeros_like(s_ref)
        ii = lax.broadcasted_iota(jnp.int32, (chunk, chunk), 0)
        jj = lax.broadcasted_iota(jnp.int32, (chunk, chunk), 1)
        causal = ii >= jj
        dist = jnp.where(causal, ii - jj, 0).astype(F32)
        for h in heads:
            decay_ref[h] = jnp.where(causal, jnp.exp(dist * log_gamma[h]), 0.0)

    pos = lax.broadcasted_iota(jnp.int32, (chunk, 1), 0).astype(F32)
    q_all, k_all, v_all, g_all = q_ref[0], k_ref[0], v_ref[0], g_ref[0]

    def head_cols(x, h):
        return x[:, h * d:(h + 1) * d]

    q = [head_cols(q_all, h) for h in heads]
    k = [head_cols(k_all, h) for h in heads]
    v = [head_cols(v_all, h) for h in heads]
    s = [s_ref[h] for h in heads]
    scores = [_dot_nt(q[h], k[h]) * decay_ref[h] for h in heads]
    cross = [_dot(q[h] * jnp.exp((pos + 1.0) * log_gamma[h]), s[h]) for h in heads]
    o = [_dot(scores[h], v[h]) + cross[h] for h in heads]
    kv = [_dot_tn(k[h] * jnp.exp((chunk - 1.0 - pos) * log_gamma[h]), v[h]) for h in heads]
    gain = nb_ref[...]
    for h in heads:
        s_ref[h] = s[h] * math.exp(chunk * log_gamma[h]) + kv[h]
        on = o[h] * lax.rsqrt(jnp.mean(o[h] * o[h], axis=-1, keepdims=True) + NORM_EPS) * gain
        o_ref[0, :, h * d:(h + 1) * d] = (on * head_cols(g_all, h)).astype(o_ref.dtype)


def _mixer_b(q, k, v, g, norm_b, n_heads):
    b, t, hd = q.shape
    d = B_HEAD_DIM
    chunk = min(B_CHUNK, t)
    hh = n_heads
    col = pl.BlockSpec((1, chunk, hd), lambda bi, ti: (bi, ti, 0))
    return pl.pallas_call(
        functools.partial(_mixer_b_kernel, chunk=chunk, n_heads=hh),
        out_shape=jax.ShapeDtypeStruct((b, t, hd), BF16),
        grid=(b, t // chunk),
        in_specs=[col, col, col, col, pl.BlockSpec((1, d), lambda bi, ti: (0, 0))],
        out_specs=col,
        scratch_shapes=[pltpu.VMEM((hh, d, d), F32), pltpu.VMEM((hh, chunk, chunk), F32)],
        compiler_params=_params(("arbitrary", "arbitrary")),
        name="mixer_retention",
    )(q, k, v, g, norm_b.reshape(1, d).astype(F32))


def _mixer_c_kernel(q_ref, k_ref, v_ref, r_ref, lr_ref, wup_ref, bgk_ref, nc_ref, o_ref, st_ref, *, chunk, n_heads):
    heads = range(n_heads)

    @pl.when(pl.program_id(1) == 0)
    def _init():
        st_ref[...] = jnp.zeros_like(st_ref)

    dk = q_ref.shape[-1] // n_heads
    dv = v_ref.shape[-1] // n_heads
    q_all, k_all, v_all, r_all = q_ref[0], k_ref[0], v_ref[0], r_ref[0]

    logit = _dot(lr_ref[0], wup_ref[...]) + bgk_ref[...]
    gk = -_softplus(-logit) / GK_NORMALIZER
    gc_all = _split_dot(_chunk_cumsum_mask(chunk, chunk), gk, 2)
    sub = min(C_SUBCHUNK, chunk)
    ii = lax.broadcasted_iota(jnp.int32, (sub, sub), 0)
    jj = lax.broadcasted_iota(jnp.int32, (sub, sub), 1)
    causal = ii >= jj

    q = [q_all[:, h * dk:(h + 1) * dk] for h in heads]
    k = [k_all[:, h * dk:(h + 1) * dk] for h in heads]
    v = [v_all[:, h * dv:(h + 1) * dv] for h in heads]
    gc = [gc_all[:, h * dk:(h + 1) * dk] for h in heads]
    g_last = [gc[h][chunk - 1:chunk] for h in heads]
    st = [st_ref[h] for h in heads]

    def score_rows(h, r0):
        rows = slice(r0, r0 + sub)
        mid = gc[h][r0 + sub // 2 - 1:r0 + sub // 2]
        diag = jnp.where(causal, _dot_nt(q[h][rows] * jnp.exp(gc[h][rows] - mid),
                                         k[h][rows] * jnp.exp(mid - gc[h][rows])), 0.0)
        parts = [diag]
        if r0 > 0:
            bnd = gc[h][r0 - 1:r0]
            parts.insert(0, _dot_nt(q[h][rows] * jnp.exp(gc[h][rows] - bnd), k[h][:r0] * jnp.exp(bnd - gc[h][:r0])))
        if r0 + sub < chunk:
            parts.append(jnp.zeros((sub, chunk - r0 - sub), F32))
        return jnp.concatenate(parts, axis=1) if len(parts) > 1 else diag

    score_blocks = [[score_rows(h, r0) for h in heads] for r0 in range(0, chunk, sub)]
    scores = [jnp.concatenate([blk[h] for blk in score_blocks], axis=0) if len(score_blocks) > 1
              else score_blocks[0][h] for h in heads]
    cross = [_dot_nt(q[h] * jnp.exp(gc[h]), st[h]) for h in heads]
    o = [_dot(scores[h], v[h]) + cross[h] for h in heads]
    kv = [_dot_tn(v[h], k[h] * jnp.exp(g_last[h] - gc[h])) for h in heads]
    gain = nc_ref[...]
    for h in heads:
        st_ref[h] = st[h] * jnp.exp(g_last[h]) + kv[h]
        on = o[h] * lax.rsqrt(jnp.mean(o[h] * o[h], axis=-1, keepdims=True) + NORM_EPS) * gain
        o_ref[0, :, h * dv:(h + 1) * dv] = (on * r_all[:, h * dv:(h + 1) * dv]).astype(o_ref.dtype)


def _mixer_c(q, k, v, r, lr, w_up, b_gk, norm_c):
    b, t, kw = q.shape
    vw = v.shape[-1]
    hh = C_HEADS
    dk, dv = kw // hh, vw // hh
    chunk = min(C_CHUNK, t)
    rw = lr.shape[-1]
    kcol = pl.BlockSpec((1, chunk, kw), lambda bi, ti: (bi, ti, 0))
    vcol = pl.BlockSpec((1, chunk, vw), lambda bi, ti: (bi, ti, 0))
    return pl.pallas_call(
        functools.partial(_mixer_c_kernel, chunk=chunk, n_heads=hh),
        out_shape=jax.ShapeDtypeStruct((b, t, vw), BF16),
        grid=(b, t // chunk),
        in_specs=[kcol, kcol, vcol, vcol,
                  pl.BlockSpec((1, chunk, rw), lambda bi, ti: (bi, ti, 0)),
                  pl.BlockSpec((rw, kw), lambda bi, ti: (0, 0)),
                  pl.BlockSpec((1, kw), lambda bi, ti: (0, 0)),
                  pl.BlockSpec((1, dv), lambda bi, ti: (0, 0))],
        out_specs=vcol,
        scratch_shapes=[pltpu.VMEM((hh, dv, dk), F32)],
        compiler_params=_params(("parallel", "arbitrary")),
        name="mixer_gla",
    )(q, k, v, r, lr, w_up, b_gk.reshape(1, kw).astype(F32), norm_c.reshape(1, dv).astype(F32))


def _pad_cols(w, width):
    return jnp.pad(w, ((0, 0), (0, width - w.shape[1])))


def _mlp(x2d, gain, w_up, w_down, layer):
    hn = _rmsnorm(x2d, gain, BF16)
    hid = _matmul(hn, w_up, layer=layer, act="relu2", out_dtype=BF16)
    return _matmul(hid, w_down, layer=layer, res=x2d, tk=4096)


def _even_layer(x2d, b, t, gain, w_in, conv_w, a_log, dt_bias, norm_a, norm_b, w_out):
    d = x2d.shape[1]
    ha, hb = d // 256, d // 512
    akw = ha * A_HEAD_DIM
    bkw = hb * B_HEAD_DIM
    small0 = 4 * akw
    b0 = small0 + 2 * ha
    conv_w = conv_w.astype(F32)

    def seq(a2d):
        return a2d.reshape(b, t, -1)

    hn = _rmsnorm(x2d, gain, BF16)
    w_bf = w_in.astype(BF16)

    def w_slice(lo, width):
        return w_bf[:, lo:lo + width]

    q_a = seq(_conv_proj(hn, w_bf, conv_w, (0, akw), t, norm_dim=A_HEAD_DIM, scale=A_HEAD_DIM ** -0.5))
    k_a = seq(_conv_proj(hn, w_bf, conv_w, (akw, akw), t, norm_dim=A_HEAD_DIM))
    v_a = seq(_conv_proj(hn, w_bf, conv_w, (2 * akw, akw), t))
    z_a = seq(_matmul(hn, w_bf, cols=(3 * akw, akw), act="silu"))
    ba = _matmul(hn, _pad_cols(w_slice(small0, 2 * ha), LANES))
    gates = _gates(ba, a_log, dt_bias, ha, A_CHUNK).reshape(b, t, LANES)
    gates_t = jnp.swapaxes(gates[:, :, ha:2 * ha], 1, 2)
    o_a = _mixer_a(q_a, k_a, v_a, z_a, gates, gates_t, norm_a, ha)
    cos, sin = _rope_tables(t, B_HEAD_DIM)
    q_b = seq(_rope_proj(hn, w_slice(b0, bkw), cos, sin, t, B_HEAD_DIM))
    k_b = seq(_rope_proj(hn, w_slice(b0 + bkw, bkw), cos, sin, t, B_HEAD_DIM, scale=B_HEAD_DIM ** -0.5))
    v_b = seq(_matmul(hn, w_slice(b0 + 2 * bkw, bkw)))
    g_b = seq(_matmul(hn, w_slice(b0 + 3 * bkw, bkw), act="silu"))
    o_b = _mixer_b(q_b, k_b, v_b, g_b, norm_b, hb)
    return _matmul(o_a.reshape(b * t, akw), w_out.astype(BF16), a2=o_b.reshape(b * t, bkw), res=x2d)


def _odd_layer(x2d, b, t, gain, w_in, w_gk_down, w_gk_up, b_gk, norm_c, w_out):
    d = x2d.shape[1]
    kw, vw = d // 2, d
    dk = kw // C_HEADS

    def seq(a2d):
        return a2d.reshape(b, t, -1)

    hn = _rmsnorm(x2d, gain, BF16)
    w_bf = w_in.astype(BF16)
    q = seq(_matmul(hn, w_bf, cols=(0, kw), act=dk ** -0.5))
    k = seq(_matmul(hn, w_bf, cols=(kw, kw)))
    v = seq(_matmul(hn, w_bf, cols=(2 * kw, vw)))
    r = seq(_matmul(hn, w_bf, cols=(2 * kw + vw, vw), act="silu"))
    lr = seq(_matmul(hn, _pad_cols(w_gk_down, LANES).astype(BF16)))
    w_up = jnp.pad(w_gk_up, ((0, LANES - w_gk_up.shape[0]), (0, 0))).astype(BF16)
    o_c = _mixer_c(q, k, v, r, lr, w_up, b_gk, norm_c).reshape(b * t, vw)
    return _matmul(o_c, w_out.astype(BF16), res=x2d)


def kernel(x, norm_mix, norm_mlp, norm_final, w_up, w_down, w_in_ab, conv_a, a_log, dt_bias, norm_a, norm_b,
           w_out_ab, w_in_c, w_gk_down, w_gk_up, b_gk, norm_c, w_out_c):
    b, t, d = x.shape
    depth = norm_mix.shape[0]
    x2d = x.reshape(b * t, d)
    w_up, w_down = w_up.astype(BF16), w_down.astype(BF16)
    for layer in range(depth):
        i = layer // 2
        if layer % 2 == 0:
            x2d = _even_layer(x2d, b, t, norm_mix[layer], w_in_ab[i], conv_a[i], a_log[i], dt_bias[i],
                              norm_a[i], norm_b[i], w_out_ab[i])
        else:
            x2d = _odd_layer(x2d, b, t, norm_mix[layer], w_in_c[i], w_gk_down[i], w_gk_up[i], b_gk[i],
                             norm_c[i], w_out_c[i])
        x2d = _mlp(x2d, norm_mlp[layer], w_up, w_down, layer)
    return _rmsnorm(x2d, norm_final, F32).reshape(b, t, d)
```

```python
import functools
import math

import jax
import jax.numpy as jnp
from jax import lax
from jax.experimental import pallas as pl
from jax.experimental.pallas import tpu as pltpu

F32 = jnp.float32
BF16 = jnp.bfloat16

NORM_EPS = 1e-6
ROPE_BASE = 10000.0
CONV_K = 4
GK_RANK = 16
GK_NORMALIZER = 16.0
A_HEAD_DIM = 128
B_HEAD_DIM = 256
C_HEADS = 4
LANES = 128
SUBLANES = 8
MXU_COLS = 256
CONV_PROJ_ROWS = 512
A_CHUNK = 128
B_CHUNK = 256
C_CHUNK = 256
C_SUBCHUNK = 128
MIB = 1024 * 1024
VMEM_MIB_WITH_STATS = 62


def _params(semantics, vmem_mib=None):
    kwargs = dict(dimension_semantics=semantics)
    if vmem_mib is not None:
        kwargs["vmem_limit_bytes"] = vmem_mib * MIB
    return pltpu.CompilerParams(**kwargs)


def _dot(a, b):
    return jnp.dot(a.astype(BF16), b.astype(BF16), preferred_element_type=F32)


def _dot_nt(a, b):
    return lax.dot_general(a.astype(BF16), b.astype(BF16), (((1,), (1,)), ((), ())),
                           preferred_element_type=F32)


def _dot_tn(a, b):
    return lax.dot_general(a.astype(BF16), b.astype(BF16), (((0,), (0,)), ((), ())),
                           preferred_element_type=F32)


def _sigmoid(x):
    return 1.0 / (1.0 + jnp.exp(-x))


def _silu(x):
    return x * _sigmoid(x)


def _softplus(x):
    return jnp.maximum(x, 0.0) + jnp.log1p(jnp.exp(-jnp.abs(x)))


def _split_dot(mask_bf16, g, pieces):
    acc = None
    rem = g
    for _ in range(pieces):
        part = rem.astype(BF16)
        term = jnp.dot(mask_bf16, part, preferred_element_type=F32)
        acc = term if acc is None else acc + term
        rem = rem - part.astype(F32)
    return acc


def _chunk_cumsum_mask(n, chunk):
    ii = lax.broadcasted_iota(jnp.int32, (n, n), 0)
    jj = lax.broadcasted_iota(jnp.int32, (n, n), 1)
    same = (ii // chunk) == (jj // chunk)
    return jnp.where(same & (ii >= jj), 1.0, 0.0).astype(BF16)


def _rmsnorm_kernel(x_ref, g_ref, o_ref):
    x = x_ref[...]
    ms = jnp.mean(x * x, axis=-1, keepdims=True)
    o_ref[...] = (x * lax.rsqrt(ms + NORM_EPS) * g_ref[...]).astype(o_ref.dtype)


def _rmsnorm(x2d, gain, out_dtype, tm=512):
    m, d = x2d.shape
    tm = min(tm, m)
    return pl.pallas_call(
        _rmsnorm_kernel,
        out_shape=jax.ShapeDtypeStruct((m, d), out_dtype),
        grid=(m // tm,),
        in_specs=[pl.BlockSpec((tm, d), lambda i: (i, 0)),
                  pl.BlockSpec((1, d), lambda i: (0, 0))],
        out_specs=pl.BlockSpec((tm, d), lambda i: (i, 0)),
        compiler_params=_params(("parallel",)),
        name="rmsnorm",
    )(x2d, gain.reshape(1, d).astype(F32))


def _matmul_kernel(*refs, nk, n_pairs, act, has_res, has_scale, emit_stats, norm_dim):
    ab_refs = refs[:2 * n_pairs]
    pos = 2 * n_pairs
    ssq_in_ref = refs[pos] if has_scale else None
    pos += int(has_scale)
    res_ref = refs[pos] if has_res else None
    pos += int(has_res)
    o_ref = refs[pos]
    xb_ref, ssq_out_ref = (refs[pos + 1], refs[pos + 2]) if emit_stats else (None, None)
    j = pl.program_id(1)

    def product():
        acc = None
        for p in range(n_pairs):
            term = jnp.dot(ab_refs[2 * p][...], ab_refs[2 * p + 1][...], preferred_element_type=F32)
            acc = term if acc is None else acc + term
        return acc

    def stats(x_new):
        xb_ref[...] = x_new.astype(xb_ref.dtype)
        part = jnp.broadcast_to(jnp.sum(x_new * x_new, axis=-1, keepdims=True), ssq_out_ref.shape)
        ssq_out_ref[...] = jnp.where(j == 0, part, ssq_out_ref[...] + part)

    if nk == 1:
        acc = product()
        if has_scale:
            acc = acc * lax.rsqrt(ssq_in_ref[:, 0:1] * (1.0 / norm_dim) + NORM_EPS)
        if act == "relu2":
            r = jnp.maximum(acc, 0.0)
            acc = r * r
        elif act == "silu":
            acc = _silu(acc)
        elif isinstance(act, float):
            acc = acc * act
        if has_res:
            acc = res_ref[...] + acc
        o_ref[...] = acc.astype(o_ref.dtype)
        if emit_stats:
            stats(acc)
    else:
        k = pl.program_id(2)

        @pl.when(k == 0)
        def _first():
            o_ref[...] = res_ref[...] if has_res else jnp.zeros_like(o_ref)

        o_ref[...] += product()
        if emit_stats:
            @pl.when(k == nk - 1)
            def _last():
                stats(o_ref[...])


def _matmul(a, b, *, layer=None, cols=None, a2=None, row_ssq=None, res=None, act=None, emit_stats=False,
            out_dtype=F32, tm=1024, tn=1024, tk=None, vmem_mib=56):
    m, ka = a.shape
    col0, n = (0, b.shape[-1]) if cols is None else cols
    n_pairs = 1 if a2 is None else 2
    tm, tn = min(tm, m), min(tn, n)
    tk = ka if tk is None else min(tk, ka)
    assert m % tm == 0 and n % tn == 0 and ka % tk == 0 and col0 % tn == 0
    jb0 = col0 // tn
    nk = ka // tk
    has_res = res is not None
    has_scale = row_ssq is not None
    assert nk == 1 or (act is None and out_dtype == F32 and a2 is None and not has_scale)
    stacked = layer is not None

    def a_map(i, j, *k):
        return (i, k[0] if k else 0)

    def o_map(i, j, *k):
        return (i, j)

    def row_map(i, j, *k):
        return (i, 0)

    def b_spec(row_block):
        def b_map(i, j, *k):
            kb = (k[0] if k else 0) + row_block
            return (layer, kb, j + jb0) if stacked else (kb, j + jb0)
        return pl.BlockSpec((None, tk, tn) if stacked else (tk, tn), b_map)

    a_spec = pl.BlockSpec((tm, tk), a_map)
    o_spec = pl.BlockSpec((tm, tn), o_map)
    ssq_spec = pl.BlockSpec((tm, LANES), row_map)
    in_specs = [a_spec, b_spec(0)]
    args = [a, b]
    if a2 is not None:
        in_specs += [a_spec, b_spec(1)]
        args += [a2, b]
    if has_scale:
        in_specs.append(ssq_spec)
        args.append(row_ssq)
    if has_res:
        in_specs.append(o_spec)
        args.append(res)
    out_shape = jax.ShapeDtypeStruct((m, n), out_dtype)
    out_specs = o_spec
    if emit_stats:
        out_shape = (out_shape, jax.ShapeDtypeStruct((m, n), BF16), jax.ShapeDtypeStruct((m, LANES), F32))
        out_specs = (o_spec, o_spec, ssq_spec)
    if emit_stats:
        vmem_mib = max(vmem_mib, VMEM_MIB_WITH_STATS)
    grid = (m // tm, n // tn) + ((nk,) if nk > 1 else ())
    sem = ("parallel", "arbitrary" if emit_stats else "parallel") + (("arbitrary",) if nk > 1 else ())
    return pl.pallas_call(
        functools.partial(_matmul_kernel, nk=nk, n_pairs=n_pairs, act=act, has_res=has_res, has_scale=has_scale,
                          emit_stats=emit_stats, norm_dim=ka),
        out_shape=out_shape,
        grid=grid,
        in_specs=in_specs,
        out_specs=out_specs,
        compiler_params=_params(sem, vmem_mib),
        name="matmul",
    )(*args)


def _conv_proj_kernel(a_ref, b_ref, cw_ref, o_ref, carry_ref, raw_ref, *, tiles_per_seq, norm_dim, scale):
    i, j = pl.program_id(0), pl.program_id(1)
    tm, tn = o_ref.shape
    rows_per_dot = min(CONV_PROJ_ROWS, tm)
    first_tile = lax.rem(i, tiles_per_seq) == 0

    prev = carry_ref[j]
    raw_ref[0:SUBLANES, :] = jnp.where(first_tile, jnp.zeros_like(prev), prev)

    def epilogue(r0, c0):
        cols = slice(c0, c0 + MXU_COLS)
        if r0 + rows_per_dot == tm:
            carry_ref[j, :, cols] = raw_ref[tm:tm + SUBLANES, cols]
        w = cw_ref[:, cols]
        y = w[CONV_K - 1:CONV_K] * raw_ref[SUBLANES + r0:SUBLANES + r0 + rows_per_dot, cols]
        for tap in range(CONV_K - 1):
            off = SUBLANES - (CONV_K - 1) + tap + r0
            y = y + w[tap:tap + 1] * raw_ref[off:off + rows_per_dot, cols]
        y = _silu(y)
        if norm_dim is not None:
            segs = []
            for s0 in range(0, MXU_COLS, norm_dim):
                seg = y[:, s0:s0 + norm_dim]
                segs.append(seg * (lax.rsqrt(jnp.sum(seg * seg, axis=-1, keepdims=True) + NORM_EPS) * scale))
            y = jnp.concatenate(segs, axis=1)
        o_ref[r0:r0 + rows_per_dot, cols] = y.astype(o_ref.dtype)

    subtiles = [(r0, c0) for r0 in range(0, tm, rows_per_dot) for c0 in range(0, tn, MXU_COLS)]
    pending = None
    for r0, c0 in subtiles:
        raw_ref[SUBLANES + r0:SUBLANES + r0 + rows_per_dot, c0:c0 + MXU_COLS] = jnp.dot(
            a_ref[r0:r0 + rows_per_dot, :], b_ref[:, c0:c0 + MXU_COLS], preferred_element_type=F32)
        if pending is not None:
            epilogue(*pending)
        pending = (r0, c0)
    epilogue(*pending)


def _conv_proj(a, b, conv_w, cols, seq_len, *, norm_dim=None, scale=1.0, tm=1024, tn=1024, vmem_mib=56):
    m, kdim = a.shape
    col0, n = cols
    tm, tn = min(tm, seq_len), min(tn, n)
    assert m % tm == 0 and n % tn == 0 and seq_len % tm == 0 and col0 % tn == 0
    jb0 = col0 // tn
    return pl.pallas_call(
        functools.partial(_conv_proj_kernel, tiles_per_seq=seq_len // tm, norm_dim=norm_dim, scale=scale),
        out_shape=jax.ShapeDtypeStruct((m, n), F32),
        grid=(m // tm, n // tn),
        in_specs=[pl.BlockSpec((tm, kdim), lambda i, j: (i, 0)),
                  pl.BlockSpec((kdim, tn), lambda i, j: (0, j + jb0)),
                  pl.BlockSpec((CONV_K, tn), lambda i, j: (0, j + jb0))],
        out_specs=pl.BlockSpec((tm, tn), lambda i, j: (i, j)),
        scratch_shapes=[pltpu.VMEM((n // tn, SUBLANES, tn), F32), pltpu.VMEM((tm + SUBLANES, tn), F32)],
        compiler_params=_params(("arbitrary", "arbitrary"), vmem_mib),
        name="conv_proj",
    )(a, b, conv_w)


def _rope_proj_kernel(a_ref, b_ref, cos_ref, sin_ref, o_ref, *, head_dim, scale):
    acc = jnp.dot(a_ref[...], b_ref[...], preferred_element_type=F32)
    tm, tn = acc.shape
    even = (lax.broadcasted_iota(jnp.int32, (tm, LANES), 1) % 2) == 0
    parts = []
    for c0 in range(0, tn, LANES):
        x = acc[:, c0:c0 + LANES]
        partner = jnp.where(even, pltpu.roll(x, LANES - 1, axis=1), pltpu.roll(x, 1, axis=1))
        t0 = c0 % head_dim
        parts.append((x * cos_ref[:, t0:t0 + LANES] + partner * sin_ref[:, t0:t0 + LANES]) * scale)
    o_ref[...] = jnp.concatenate(parts, axis=1).astype(o_ref.dtype)


def _rope_proj(a, b, cos, sin, seq_len, head_dim, *, scale=1.0, tm=1024, tn=1024, vmem_mib=56):
    m, kdim = a.shape
    n = b.shape[1]
    tm, tn = min(tm, seq_len), min(tn, n)
    assert m % tm == 0 and n % tn == 0 and seq_len % tm == 0 and tn % head_dim == 0
    tiles_per_seq = seq_len // tm
    tab = pl.BlockSpec((tm, head_dim), lambda i, j: (i % tiles_per_seq, 0))
    return pl.pallas_call(
        functools.partial(_rope_proj_kernel, head_dim=head_dim, scale=scale),
        out_shape=jax.ShapeDtypeStruct((m, n), F32),
        grid=(m // tm, n // tn),
        in_specs=[pl.BlockSpec((tm, kdim), lambda i, j: (i, 0)),
                  pl.BlockSpec((kdim, tn), lambda i, j: (0, j)),
                  tab, tab],
        out_specs=pl.BlockSpec((tm, tn), lambda i, j: (i, j)),
        compiler_params=_params(("parallel", "parallel"), vmem_mib),
        name="rope_proj",
    )(a, b, cos, sin)


def _rope_kernel(inv_ref, sign_ref, cos_ref, sin_ref, *, tt):
    pos = (lax.broadcasted_iota(jnp.int32, cos_ref.shape, 0) + pl.program_id(0) * tt).astype(F32)
    ang = pos * inv_ref[...]
    cos_ref[...] = jnp.cos(ang)
    sin_ref[...] = jnp.sin(ang) * sign_ref[...]


def _rope_tables(t, head_dim, tt=512):
    tt = min(tt, t)
    half = head_dim // 2
    inv_freq = jnp.power(ROPE_BASE, -jnp.linspace(0.0, 1.0, half, dtype=F32))
    inv_pair = jnp.repeat(inv_freq, 2).reshape(1, head_dim)
    sign = jnp.tile(jnp.array([-1.0, 1.0], F32), half).reshape(1, head_dim)
    row = pl.BlockSpec((1, head_dim), lambda i: (0, 0))
    tab = pl.BlockSpec((tt, head_dim), lambda i: (i, 0))
    return pl.pallas_call(
        functools.partial(_rope_kernel, tt=tt),
        out_shape=(jax.ShapeDtypeStruct((t, head_dim), F32), jax.ShapeDtypeStruct((t, head_dim), F32)),
        grid=(t // tt,),
        in_specs=[row, row],
        out_specs=(tab, tab),
        compiler_params=_params(("parallel",)),
        name="rope_tables",
    )(inv_pair, sign)


def _gates_kernel(x_ref, alog_ref, dtb_ref, o_ref, *, n_heads, chunk):
    x = x_ref[...]
    beta = _sigmoid(x)
    g = -jnp.exp(alog_ref[...]) * _softplus(x + dtb_ref[...])
    gc = _split_dot(_chunk_cumsum_mask(x.shape[0], chunk), g, 3)
    lane = lax.broadcasted_iota(jnp.int32, x.shape, 1)
    o_ref[...] = jnp.where(lane < n_heads, beta, gc)


def _gates(ba, a_log, dt_bias, n_heads, chunk, tg=256):
    m, w = ba.shape
    tg = min(tg, m)
    alog_p = jnp.zeros((1, w), F32).at[0, n_heads:2 * n_heads].set(a_log.astype(F32))
    dtb_p = jnp.zeros((1, w), F32).at[0, n_heads:2 * n_heads].set(dt_bias.astype(F32))
    return pl.pallas_call(
        functools.partial(_gates_kernel, n_heads=n_heads, chunk=chunk),
        out_shape=jax.ShapeDtypeStruct((m, w), F32),
        grid=(m // tg,),
        in_specs=[pl.BlockSpec((tg, w), lambda i: (i, 0)),
                  pl.BlockSpec((1, w), lambda i: (0, 0)),
                  pl.BlockSpec((1, w), lambda i: (0, 0))],
        out_specs=pl.BlockSpec((tg, w), lambda i: (i, 0)),
        compiler_params=_params(("parallel",)),
        name="deltanet_gates",
    )(ba, alog_p, dtb_p)


def _unit_lower_inverses(mats, n):
    ii = lax.broadcasted_iota(jnp.int32, (n, n), 0)
    jj = lax.broadcasted_iota(jnp.int32, (n, n), 1)
    base = 16
    eye = jnp.where(ii == jj, 1.0, 0.0)
    diag_blocks = (ii // base) == (jj // base)
    ps = [jnp.where(diag_blocks, -a, 0.0) for a in mats]
    ts = [eye + p for p in ps]
    width = 2
    while width < base:
        ps = [_dot(p, p) for p in ps]
        ts = [t + _dot(t, p) for t, p in zip(ts, ps)]
        width *= 2
    bs = base
    while bs < n:
        off_blocks = ((ii // (2 * bs)) == (jj // (2 * bs))) & ((ii // bs) != (jj // bs))
        xs = [_dot(t, jnp.where(off_blocks, a, 0.0)) for t, a in zip(ts, mats)]
        ts = [t - _dot(x, t) for t, x in zip(ts, xs)]
        bs *= 2
    return ts


def _mixer_a_kernel(q_ref, k_ref, v_ref, z_ref, gate_ref, gct_ref, na_ref, o_ref, s_ref, *, tb, n_heads):
    d = A_HEAD_DIM
    heads = range(n_heads)

    @pl.when(pl.program_id(1) == 0)
    def _init():
        s_ref[...] = jnp.zeros_like(s_ref)

    q_all, k_all, v_all, z_all = q_ref[0], k_ref[0], v_ref[0], z_ref[0]
    gt = gate_ref[0]
    lane = lax.broadcasted_iota(jnp.int32, gt.shape, 1)

    ii = lax.broadcasted_iota(jnp.int32, (tb, tb), 0)
    jj = lax.broadcasted_iota(jnp.int32, (tb, tb), 1)
    causal = ii >= jj
    strict = ii > jj
    gain = na_ref[...]

    def head_cols(x, h):
        return x[:, h * d:(h + 1) * d]

    q = [head_cols(q_all, h) for h in heads]
    k = [head_cols(k_all, h) for h in heads]
    beta =[jnp.sum(jnp.where(lane == h, gt, 0.0), axis=1, keepdims=True) for h in heads]
    gc = [jnp.sum(jnp.where(lane == h + n_heads, gt, 0.0), axis=1, keepdims=True) for h in heads]
    grow = gct_ref[0]
    decay = [jnp.where(causal, jnp.exp(jnp.where(causal, gc[h] - grow[h:h + 1], 0.0)), 0.0) for h in heads]
    kb = [k[h] * beta[h] for h in heads]
    eg = [jnp.exp(gc[h]) for h in heads]

    kq = [_dot_nt(jnp.concatenate([kb[h], q[h]], axis=0), k[h]) for h in heads]
    a = [jnp.where(strict, kq[h][:tb] * decay[h], 0.0) for h in heads]
    scores = [kq[h][tb:] * decay[h] for h in heads]
    t = _unit_lower_inverses(a, tb)
    uw = [_dot(t[h], jnp.concatenate([head_cols(v_all, h) * beta[h], kb[h] * eg[h]], axis=1)) for h in heads]
    s = [s_ref[h] for h in heads]
    ws = [_dot(jnp.concatenate([uw[h][:, d:], q[h] * eg[h]], axis=0), s[h]) for h in heads]
    v_new = [uw[h][:, :d] - ws[h][:tb] for h in heads]
    o = [ws[h][tb:] + _dot(scores[h], v_new[h]) for h in heads]
    g_last = [gc[h][tb - 1:tb] for h in heads]
    kv = [_dot_tn(k[h] * jnp.exp(g_last[h] - gc[h]), v_new[h]) for h in heads]
    for h in heads:
        s_ref[h] = s[h] * jnp.exp(g_last[h]) + kv[h]
        on = o[h] * lax.rsqrt(jnp.mean(o[h] * o[h], axis=-1, keepdims=True) + NORM_EPS) * gain
        o_ref[0, :, h * d:(h + 1) * d] = (on * head_cols(z_all, h)).astype(o_ref.dtype)


def _mixer_a(q, k, v, z, gates, gates_t, norm_a, n_heads):
    b, t, hd = q.shape
    d = A_HEAD_DIM
    tb = min(A_CHUNK, t)
    hh = n_heads
    col = pl.BlockSpec((1, tb, hd), lambda bi, ti: (bi, ti, 0))
    return pl.pallas_call(
        functools.partial(_mixer_a_kernel, tb=tb, n_heads=hh),
        out_shape=jax.ShapeDtypeStruct((b, t, hd), BF16),
        grid=(b, t // tb),
        in_specs=[col, col, col, col,
                  pl.BlockSpec((1, tb, gates.shape[-1]), lambda bi, ti: (bi, ti, 0)),
                  pl.BlockSpec((1, hh, tb), lambda bi, ti: (bi, 0, ti)),
                  pl.BlockSpec((1, d), lambda bi, ti: (0, 0))],
        out_specs=col,
        scratch_shapes=[pltpu.VMEM((hh, d, d), F32)],
        compiler_params=_params(("parallel", "arbitrary")),
        name="mixer_deltanet",
    )(q, k, v, z, gates, gates_t, norm_a.reshape(1, d).astype(F32))


def _mixer_b_kernel(q_ref, k_ref, v_ref, g_ref, nb_ref, o_ref, s_ref, decay_ref, *, chunk, n_heads):
    d = B_HEAD_DIM
    heads = range(n_heads)
    log_gamma = [math.log1p(-(2.0 ** (-5.0 - h))) for h in heads]

    @pl.when(pl.program_id(1) == 0)
    def _init():
        s_ref[...] = jnp.zeros_like(s_ref)
        ii = lax.broadcasted_iota(jnp.int32, (chunk, chunk), 0)
        jj = lax.broadcasted_iota(jnp.int32, (chunk, chunk), 1)
        causal = ii >= jj
        dist = jnp.where(causal, ii - jj, 0).astype(F32)
        for h in heads:
            decay_ref[h] = jnp.where(causal, jnp.exp(dist * log_gamma[h]), 0.0)

    pos = lax.broadcasted_iota(jnp.int32, (chunk, 1), 0).astype(F32)
    q_all, k_all, v_all, g_all = q_ref[0], k_ref[0], v_ref[0], g_ref[0]

    def head_cols(x, h):
        return x[:, h * d:(h + 1) * d]

    q = [head_cols(q_all, h) for h in heads]
    k = [head_cols(k_all, h) for h in heads]
    v = [head_cols(v_all, h) for h in heads]
    s = [s_ref[h] for h in heads]
    scores = [_dot_nt(q[h], k[h]) * decay_ref[h] for h in heads]
    cross = [_dot(q[h] * jnp.exp((pos + 1.0) * log_gamma[h]), s[h]) for h in heads]
    o = [_dot(scores[h], v[h]) + cross[h] for h in heads]
    kv = [_dot_tn(k[h] * jnp.exp((chunk - 1.0 - pos) * log_gamma[h]), v[h]) for h in heads]
    gain = nb_ref[...]
    for h in heads:
        s_ref[h] = s[h] * math.exp(chunk * log_gamma[h]) + kv[h]
        on = o[h] * lax.rsqrt(jnp.mean(o[h] * o[h], axis=-1, keepdims=True) + NORM_EPS) * gain
        o_ref[0, :, h * d:(h + 1) * d] = (on * head_cols(g_all, h)).astype(o_ref.dtype)


def _mixer_b(q, k, v, g, norm_b, n_heads):
    b, t, hd = q.shape
    d = B_HEAD_DIM
    chunk = min(B_CHUNK, t)
    hh = n_heads
    col = pl.BlockSpec((1, chunk, hd), lambda bi, ti: (bi, ti, 0))
    return pl.pallas_call(
        functools.partial(_mixer_b_kernel, chunk=chunk, n_heads=hh),
        out_shape=jax.ShapeDtypeStruct((b, t, hd), BF16),
        grid=(b, t // chunk),
        in_specs=[col, col, col, col, pl.BlockSpec((1, d), lambda bi, ti: (0, 0))],
        out_specs=col,
        scratch_shapes=[pltpu.VMEM((hh, d, d), F32), pltpu.VMEM((hh, chunk, chunk), F32)],
        compiler_params=_params(("arbitrary", "arbitrary")),
        name="mixer_retention",
    )(q, k, v, g, norm_b.reshape(1, d).astype(F32))


def _mixer_c_kernel(q_ref, k_ref, v_ref, r_ref, lr_ref, wup_ref, bgk_ref, nc_ref, o_ref, st_ref, *, chunk, n_heads):
    heads = range(n_heads)

    @pl.when(pl.program_id(1) == 0)
    def _init():
        st_ref[...] = jnp.zeros_like(st_ref)

    dk = q_ref.shape[-1] // n_heads
    dv = v_ref.shape[-1] // n_heads
    q_all, k_all, v_all, r_all = q_ref[0], k_ref[0], v_ref[0], r_ref[0]

    logit = _dot(lr_ref[0], wup_ref[...]) + bgk_ref[...]
    gk = -_softplus(-logit) / GK_NORMALIZER
    gc_all = _split_dot(_chunk_cumsum_mask(chunk, chunk), gk, 2)
    sub = min(C_SUBCHUNK, chunk)
    ii = lax.broadcasted_iota(jnp.int32, (sub, sub), 0)
    jj = lax.broadcasted_iota(jnp.int32, (sub, sub), 1)
    causal = ii >= jj

    q = [q_all[:, h * dk:(h + 1) * dk] for h in heads]
    k = [k_all[:, h * dk:(h + 1) * dk] for h in heads]
    v = [v_all[:, h * dv:(h + 1) * dv] for h in heads]
    gc = [gc_all[:, h * dk:(h + 1) * dk] for h in heads]
    g_last = [gc[h][chunk - 1:chunk] for h in heads]
    st = [st_ref[h] for h in heads]

    def score_rows(h, r0):
        rows = slice(r0, r0 + sub)
        mid = gc[h][r0 + sub // 2 - 1:r0 + sub // 2]
        diag = jnp.where(causal, _dot_nt(q[h][rows] * jnp.exp(gc[h][rows] - mid),
                                         k[h][rows] * jnp.exp(mid - gc[h][rows])), 0.0)
        parts = [diag]
        if r0 > 0:
            bnd = gc[h][r0 - 1:r0]
            parts.insert(0, _dot_nt(q[h][rows] * jnp.exp(gc[h][rows] - bnd), k[h][:r0] * jnp.exp(bnd - gc[h][:r0])))
        if r0 + sub < chunk:
            parts.append(jnp.zeros((sub, chunk - r0 - sub), F32))
        return jnp.concatenate(parts, axis=1) if len(parts) > 1 else diag

    score_blocks = [[score_rows(h, r0) for h in heads] for r0 in range(0, chunk, sub)]
    scores = [jnp.concatenate([blk[h] for blk in score_blocks], axis=0) if len(score_blocks) > 1
              else score_blocks[0][h] for h in heads]
    cross = [_dot_nt(q[h] * jnp.exp(gc[h]), st[h]) for h in heads]
    o = [_dot(scores[h], v[h]) + cross[h] for h in heads]
    kv = [_dot_tn(v[h], k[h] * jnp.exp(g_last[h] - gc[h])) for h in heads]
    gain = nc_ref[...]
    for h in heads:
        st_ref[h] = st[h] * jnp.exp(g_last[h]) + kv[h]
        on = o[h] * lax.rsqrt(jnp.mean(o[h] * o[h], axis=-1, keepdims=True) + NORM_EPS) * gain
        o_ref[0, :, h * dv:(h + 1) * dv] = (on * r_all[:, h * dv:(h + 1) * dv]).astype(o_ref.dtype)


def _mixer_c(q, k, v, r, lr, w_up, b_gk, norm_c):
    b, t, kw = q.shape
    vw = v.shape[-1]
    hh = C_HEADS
    dk, dv = kw // hh, vw // hh
    chunk = min(C_CHUNK, t)
    rw = lr.shape[-1]
    kcol = pl.BlockSpec((1, chunk, kw), lambda bi, ti: (bi, ti, 0))
    vcol = pl.BlockSpec((1, chunk, vw), lambda bi, ti: (bi, ti, 0))
    return pl.pallas_call(
        functools.partial(_mixer_c_kernel, chunk=chunk, n_heads=hh),
        out_shape=jax.ShapeDtypeStruct((b, t, vw), BF16),
        grid=(b, t // chunk),
        in_specs=[kcol, kcol, vcol, vcol,
                  pl.BlockSpec((1, chunk, rw), lambda bi, ti: (bi, ti, 0)),
                  pl.BlockSpec((rw, kw), lambda bi, ti: (0, 0)),
                  pl.BlockSpec((1, kw), lambda bi, ti: (0, 0)),
                  pl.BlockSpec((1, dv), lambda bi, ti: (0, 0))],
        out_specs=vcol,
        scratch_shapes=[pltpu.VMEM((hh, dv, dk), F32)],
        compiler_params=_params(("parallel", "arbitrary")),
        name="mixer_gla",
    )(q, k, v, r, lr, w_up, b_gk.reshape(1, kw).astype(F32), norm_c.reshape(1, dv).astype(F32))


def _pad_cols(w, width):
    return jnp.pad(w, ((0, 0), (0, width - w.shape[1])))


def _mlp(x2d, xb, ssq, w_up_g, w_down, layer, emit_stats):
    hid = _matmul(xb, w_up_g, layer=layer, row_ssq=ssq, act="relu2", out_dtype=BF16)
    return _matmul(hid, w_down, layer=layer, res=x2d, tk=4096, emit_stats=emit_stats)


def _even_layer(x2d, b, t, gain, w_in, conv_w, a_log, dt_bias, norm_a, norm_b, w_out):
    d = x2d.shape[1]
    ha, hb = d // 256, d // 512
    akw = ha * A_HEAD_DIM
    bkw = hb * B_HEAD_DIM
    small0 = 4 * akw
    b0 = small0 + 2 * ha
    conv_w = conv_w.astype(F32)

    def seq(a2d):
        return a2d.reshape(b, t, -1)

    hn = _rmsnorm(x2d, gain, BF16)
    w_bf = w_in.astype(BF16)

    def w_slice(lo, width):
        return w_bf[:, lo:lo + width]

    q_a = seq(_conv_proj(hn, w_bf, conv_w, (0, akw), t, norm_dim=A_HEAD_DIM, scale=A_HEAD_DIM ** -0.5))
    k_a = seq(_conv_proj(hn, w_bf, conv_w, (akw, akw), t, norm_dim=A_HEAD_DIM))
    v_a = seq(_conv_proj(hn, w_bf, conv_w, (2 * akw, akw), t))
    z_a = seq(_matmul(hn, w_bf, cols=(3 * akw, akw), act="silu"))
    ba = _matmul(hn, _pad_cols(w_slice(small0, 2 * ha), LANES))
    gates = _gates(ba, a_log, dt_bias, ha, A_CHUNK).reshape(b, t, LANES)
    gates_t = jnp.swapaxes(gates[:, :, ha:2 * ha], 1, 2)
    o_a = _mixer_a(q_a, k_a, v_a, z_a, gates, gates_t, norm_a, ha)
    cos, sin = _rope_tables(t, B_HEAD_DIM)
    q_b = seq(_rope_proj(hn, w_slice(b0, bkw), cos, sin, t, B_HEAD_DIM))
    k_b = seq(_rope_proj(hn, w_slice(b0 + bkw, bkw), cos, sin, t, B_HEAD_DIM, scale=B_HEAD_DIM ** -0.5))
    v_b = seq(_matmul(hn, w_slice(b0 + 2 * bkw, bkw)))
    g_b = seq(_matmul(hn, w_slice(b0 + 3 * bkw, bkw), act="silu"))
    o_b = _mixer_b(q_b, k_b, v_b, g_b, norm_b, hb)
    return _matmul(o_a.reshape(b * t, akw), w_out.astype(BF16), a2=o_b.reshape(b * t, bkw), res=x2d,
                   emit_stats=True)


def _odd_layer(x2d, xb, ssq, b, t, gain, w_in, w_gk_down, w_gk_up, b_gk, norm_c, w_out):
    d = x2d.shape[1]
    kw, vw = d // 2, d
    dk = kw // C_HEADS

    def seq(a2d):
        return a2d.reshape(b, t, -1)

    gcol = gain.astype(F32)[:, None]
    w_bf = (w_in * gcol).astype(BF16)
    q = seq(_matmul(xb, w_bf, cols=(0, kw), row_ssq=ssq, act=dk ** -0.5))
    k = seq(_matmul(xb, w_bf, cols=(kw, kw), row_ssq=ssq))
    v = seq(_matmul(xb, w_bf, cols=(2 * kw, vw), row_ssq=ssq))
    r = seq(_matmul(xb, w_bf, cols=(2 * kw + vw, vw), row_ssq=ssq, act="silu"))
    lr = seq(_matmul(xb, _pad_cols(w_gk_down * gcol, LANES).astype(BF16), row_ssq=ssq))
    w_up = jnp.pad(w_gk_up, ((0, LANES - w_gk_up.shape[0]), (0, 0))).astype(BF16)
    o_c = _mixer_c(q, k, v, r, lr, w_up, b_gk, norm_c).reshape(b * t, vw)
    return _matmul(o_c, w_out.astype(BF16), res=x2d, emit_stats=True)


def kernel(x, norm_mix, norm_mlp, norm_final, w_up, w_down, w_in_ab, conv_a, a_log, dt_bias, norm_a, norm_b,
           w_out_ab, w_in_c, w_gk_down, w_gk_up, b_gk, norm_c, w_out_c):
    b, t, d = x.shape
    depth = norm_mix.shape[0]
    x2d = x.reshape(b * t, d)
    w_up_g = (w_up * norm_mlp.astype(F32)[:, :, None]).astype(BF16)
    w_down = w_down.astype(BF16)
    xb = ssq = None
    for layer in range(depth):
        i = layer // 2
        if layer % 2 == 0:
            x2d, xb, ssq = _even_layer(x2d, b, t, norm_mix[layer], w_in_ab[i], conv_a[i], a_log[i], dt_bias[i],
                                       norm_a[i], norm_b[i], w_out_ab[i])
        else:
            x2d, xb, ssq = _odd_layer(x2d, xb, ssq, b, t, norm_mix[layer], w_in_c[i], w_gk_down[i], w_gk_up[i],
                                      b_gk[i], norm_c[i], w_out_c[i])
        if layer + 1 < depth:
            x2d, xb, ssq = _mlp(x2d, xb, ssq, w_up_g, w_down, layer, emit_stats=True)
        else:
            x2d = _mlp(x2d, xb, ssq, w_up_g, w_down, layer, emit_stats=False)
    return _rmsnorm(x2d, norm_final, F32).reshape(b, t, d)
```

```python
import functools
import math

import jax
import jax.numpy as jnp
from jax import lax
from jax.experimental import pallas as pl
from jax.experimental.pallas import tpu as pltpu

F32 = jnp.float32
BF16 = jnp.bfloat16

NORM_EPS = 1e-6
ROPE_BASE = 10000.0
CONV_K = 4
GK_NORMALIZER = 16.0
A_HEAD_DIM = 128
B_HEAD_DIM = 256
C_HEADS = 4
LANES = 128
SUBLANES = 8
MXU_COLS = 256
CONV_PROJ_ROWS = 512
A_CHUNK = 128
B_CHUNK = 256
C_CHUNK = 256
C_SUBCHUNK = 128
MIB = 1024 * 1024
VMEM_MIB_WITH_STATS = 62


def _params(semantics, vmem_mib=None):
    kwargs = dict(dimension_semantics=semantics)
    if vmem_mib is not None:
        kwargs["vmem_limit_bytes"] = vmem_mib * MIB
    return pltpu.CompilerParams(**kwargs)


def _dot(a, b):
    return jnp.dot(a.astype(BF16), b.astype(BF16), preferred_element_type=F32)


def _dot_nt(a, b):
    return lax.dot_general(a.astype(BF16), b.astype(BF16), (((1,), (1,)), ((), ())),
                           preferred_element_type=F32)


def _dot_tn(a, b):
    return lax.dot_general(a.astype(BF16), b.astype(BF16), (((0,), (0,)), ((), ())),
                           preferred_element_type=F32)


def _sigmoid(x):
    return 1.0 / (1.0 + jnp.exp(-x))


def _silu(x):
    return x * _sigmoid(x)


def _softplus(x):
    return jnp.maximum(x, 0.0) + jnp.log1p(jnp.exp(-jnp.abs(x)))


def _split_dot(mask_bf16, g, pieces):
    acc = None
    rem = g
    for _ in range(pieces):
        part = rem.astype(BF16)
        term = jnp.dot(mask_bf16, part, preferred_element_type=F32)
        acc = term if acc is None else acc + term
        rem = rem - part.astype(F32)
    return acc


def _chunk_cumsum_mask(n, chunk):
    ii = lax.broadcasted_iota(jnp.int32, (n, n), 0)
    jj = lax.broadcasted_iota(jnp.int32, (n, n), 1)
    same = (ii // chunk) == (jj // chunk)
    return jnp.where(same & (ii >= jj), 1.0, 0.0).astype(BF16)


def _rmsnorm_kernel(x_ref, g_ref, o_ref):
    x = x_ref[...]
    ms = jnp.mean(x * x, axis=-1, keepdims=True)
    o_ref[...] = (x * lax.rsqrt(ms + NORM_EPS) * g_ref[...]).astype(o_ref.dtype)


def _rmsnorm(x2d, gain, out_dtype, tm=512):
    m, d = x2d.shape
    tm = min(tm, m)
    return pl.pallas_call(
        _rmsnorm_kernel,
        out_shape=jax.ShapeDtypeStruct((m, d), out_dtype),
        grid=(m // tm,),
        in_specs=[pl.BlockSpec((tm, d), lambda i: (i, 0)),
                  pl.BlockSpec((1, d), lambda i: (0, 0))],
        out_specs=pl.BlockSpec((tm, d), lambda i: (i, 0)),
        compiler_params=_params(("parallel",)),
        name="rmsnorm",
    )(x2d, gain.reshape(1, d).astype(F32))


def _matmul_kernel(*refs, nk, n_pairs, act, has_res, has_scale, emit_stats, norm_dim):
    ab_refs = refs[:2 * n_pairs]
    pos = 2 * n_pairs
    ssq_in_ref = refs[pos] if has_scale else None
    pos += int(has_scale)
    res_ref = refs[pos] if has_res else None
    pos += int(has_res)
    o_ref = refs[pos]
    xb_ref, ssq_out_ref = (refs[pos + 1], refs[pos + 2]) if emit_stats else (None, None)
    j = pl.program_id(1)

    def product():
        acc = None
        for p in range(n_pairs):
            term = jnp.dot(ab_refs[2 * p][...], ab_refs[2 * p + 1][...], preferred_element_type=F32)
            acc = term if acc is None else acc + term
        return acc

    def stats(x_new):
        xb_ref[...] = x_new.astype(xb_ref.dtype)
        part = jnp.broadcast_to(jnp.sum(x_new * x_new, axis=-1, keepdims=True), ssq_out_ref.shape)
        ssq_out_ref[...] = jnp.where(j == 0, part, ssq_out_ref[...] + part)

    if nk == 1:
        acc = product()
        if has_scale:
            acc = acc * lax.rsqrt(ssq_in_ref[:, 0:1] * (1.0 / norm_dim) + NORM_EPS)
        if act == "relu2":
            r = jnp.maximum(acc, 0.0)
            acc = r * r
        elif act == "silu":
            acc = _silu(acc)
        elif isinstance(act, float):
            acc = acc * act
        if has_res:
            acc = res_ref[...] + acc
        o_ref[...] = acc.astype(o_ref.dtype)
        if emit_stats:
            stats(acc)
    else:
        k = pl.program_id(2)

        @pl.when(k == 0)
        def _first():
            o_ref[...] = res_ref[...] if has_res else jnp.zeros_like(o_ref)

        o_ref[...] += product()
        if emit_stats:
            @pl.when(k == nk - 1)
            def _last():
                stats(o_ref[...])


def _matmul(a, b, *, layer=None, cols=None, a2=None, row_ssq=None, res=None, act=None, emit_stats=False,
            out_dtype=F32, tm=1024, tn=1024, tk=None, vmem_mib=56):
    m, ka = a.shape
    col0, n = (0, b.shape[-1]) if cols is None else cols
    n_pairs = 1 if a2 is None else 2
    tm, tn = min(tm, m), min(tn, n)
    tk = ka if tk is None else min(tk, ka)
    assert m % tm == 0 and n % tn == 0 and ka % tk == 0 and col0 % tn == 0
    jb0 = col0 // tn
    nk = ka // tk
    has_res = res is not None
    has_scale = row_ssq is not None
    assert nk == 1 or (act is None and out_dtype == F32 and a2 is None and not has_scale)
    stacked = layer is not None

    def a_map(i, j, *k):
        return (i, k[0] if k else 0)

    def o_map(i, j, *k):
        return (i, j)

    def row_map(i, j, *k):
        return (i, 0)

    def b_spec(row_block):
        def b_map(i, j, *k):
            kb = (k[0] if k else 0) + row_block
            return (layer, kb, j + jb0) if stacked else (kb, j + jb0)
        return pl.BlockSpec((None, tk, tn) if stacked else (tk, tn), b_map)

    a_spec = pl.BlockSpec((tm, tk), a_map)
    o_spec = pl.BlockSpec((tm, tn), o_map)
    ssq_spec = pl.BlockSpec((tm, LANES), row_map)
    in_specs = [a_spec, b_spec(0)]
    args = [a, b]
    if a2 is not None:
        in_specs += [a_spec, b_spec(1)]
        args += [a2, b]
    if has_scale:
        in_specs.append(ssq_spec)
        args.append(row_ssq)
    if has_res:
        in_specs.append(o_spec)
        args.append(res)
    out_shape = jax.ShapeDtypeStruct((m, n), out_dtype)
    out_specs = o_spec
    if emit_stats:
        out_shape = (out_shape, jax.ShapeDtypeStruct((m, n), BF16), jax.ShapeDtypeStruct((m, LANES), F32))
        out_specs = (o_spec, o_spec, ssq_spec)
    if emit_stats:
        vmem_mib = max(vmem_mib, VMEM_MIB_WITH_STATS)
    grid = (m // tm, n // tn) + ((nk,) if nk > 1 else ())
    sem = ("parallel", "arbitrary" if emit_stats else "parallel") + (("arbitrary",) if nk > 1 else ())
    return pl.pallas_call(
        functools.partial(_matmul_kernel, nk=nk, n_pairs=n_pairs, act=act, has_res=has_res, has_scale=has_scale,
                          emit_stats=emit_stats, norm_dim=ka),
        out_shape=out_shape,
        grid=grid,
        in_specs=in_specs,
        out_specs=out_specs,
        compiler_params=_params(sem, vmem_mib),
        name="matmul",
    )(*args)


def _conv_proj_kernel(a_ref, b_ref, cw_ref, o_ref, carry_ref, raw_ref, *, tiles_per_seq, norm_dim, scale):
    i, j = pl.program_id(0), pl.program_id(1)
    tm, tn = o_ref.shape
    rows_per_dot = min(CONV_PROJ_ROWS, tm)
    first_tile = lax.rem(i, tiles_per_seq) == 0

    prev = carry_ref[j]
    raw_ref[0:SUBLANES, :] = jnp.where(first_tile, jnp.zeros_like(prev), prev)

    def epilogue(r0, c0):
        cols = slice(c0, c0 + MXU_COLS)
        if r0 + rows_per_dot == tm:
            carry_ref[j, :, cols] = raw_ref[tm:tm + SUBLANES, cols]
        w = cw_ref[:, cols]
        y = w[CONV_K - 1:CONV_K] * raw_ref[SUBLANES + r0:SUBLANES + r0 + rows_per_dot, cols]
        for tap in range(CONV_K - 1):
            off = SUBLANES - (CONV_K - 1) + tap + r0
            y = y + w[tap:tap + 1] * raw_ref[off:off + rows_per_dot, cols]
        y = _silu(y)
        if norm_dim is not None:
            segs = []
            for s0 in range(0, MXU_COLS, norm_dim):
                seg = y[:, s0:s0 + norm_dim]
                segs.append(seg * (lax.rsqrt(jnp.sum(seg * seg, axis=-1, keepdims=True) + NORM_EPS) * scale))
            y = jnp.concatenate(segs, axis=1)
        o_ref[r0:r0 + rows_per_dot, cols] = y.astype(o_ref.dtype)

    subtiles = [(r0, c0) for r0 in range(0, tm, rows_per_dot) for c0 in range(0, tn, MXU_COLS)]
    pending = None
    for r0, c0 in subtiles:
        raw_ref[SUBLANES + r0:SUBLANES + r0 + rows_per_dot, c0:c0 + MXU_COLS] = jnp.dot(
            a_ref[r0:r0 + rows_per_dot, :], b_ref[:, c0:c0 + MXU_COLS], preferred_element_type=F32)
        if pending is not None:
            epilogue(*pending)
        pending = (r0, c0)
    epilogue(*pending)


def _conv_proj(a, b, conv_w, cols, seq_len, *, norm_dim=None, scale=1.0, tm=1024, tn=1024, vmem_mib=56):
    m, kdim = a.shape
    col0, n = cols
    tm, tn = min(tm, seq_len), min(tn, n)
    assert m % tm == 0 and n % tn == 0 and seq_len % tm == 0 and col0 % tn == 0
    jb0 = col0 // tn
    return pl.pallas_call(
        functools.partial(_conv_proj_kernel, tiles_per_seq=seq_len // tm, norm_dim=norm_dim, scale=scale),
        out_shape=jax.ShapeDtypeStruct((m, n), F32),
        grid=(m // tm, n // tn),
        in_specs=[pl.BlockSpec((tm, kdim), lambda i, j: (i, 0)),
                  pl.BlockSpec((kdim, tn), lambda i, j: (0, j + jb0)),
                  pl.BlockSpec((CONV_K, tn), lambda i, j: (0, j + jb0))],
        out_specs=pl.BlockSpec((tm, tn), lambda i, j: (i, j)),
        scratch_shapes=[pltpu.VMEM((n // tn, SUBLANES, tn), F32), pltpu.VMEM((tm + SUBLANES, tn), F32)],
        compiler_params=_params(("arbitrary", "arbitrary"), vmem_mib),
        name="conv_proj",
    )(a, b, conv_w)


def _rope_proj_kernel(a_ref, b_ref, cos_ref, sin_ref, o_ref, *, head_dim, scale):
    acc = jnp.dot(a_ref[...], b_ref[...], preferred_element_type=F32)
    tm, tn = acc.shape
    even = (lax.broadcasted_iota(jnp.int32, (tm, LANES), 1) % 2) == 0
    parts = []
    for c0 in range(0, tn, LANES):
        x = acc[:, c0:c0 + LANES]
        partner = jnp.where(even, pltpu.roll(x, LANES - 1, axis=1), pltpu.roll(x, 1, axis=1))
        t0 = c0 % head_dim
        parts.append((x * cos_ref[:, t0:t0 + LANES] + partner * sin_ref[:, t0:t0 + LANES]) * scale)
    o_ref[...] = jnp.concatenate(parts, axis=1).astype(o_ref.dtype)


def _rope_proj(a, b, cos, sin, seq_len, head_dim, *, scale=1.0, tm=1024, tn=1024, vmem_mib=56):
    m, kdim = a.shape
    n = b.shape[1]
    tm, tn = min(tm, seq_len), min(tn, n)
    assert m % tm == 0 and n % tn == 0 and seq_len % tm == 0 and tn % head_dim == 0
    tiles_per_seq = seq_len // tm
    tab = pl.BlockSpec((tm, head_dim), lambda i, j: (i % tiles_per_seq, 0))
    return pl.pallas_call(
        functools.partial(_rope_proj_kernel, head_dim=head_dim, scale=scale),
        out_shape=jax.ShapeDtypeStruct((m, n), F32),
        grid=(m // tm, n // tn),
        in_specs=[pl.BlockSpec((tm, kdim), lambda i, j: (i, 0)),
                  pl.BlockSpec((kdim, tn), lambda i, j: (0, j)),
                  tab, tab],
        out_specs=pl.BlockSpec((tm, tn), lambda i, j: (i, j)),
        compiler_params=_params(("parallel", "parallel"), vmem_mib),
        name="rope_proj",
    )(a, b, cos, sin)


def _rope_kernel(inv_ref, sign_ref, cos_ref, sin_ref, *, tt):
    pos = (lax.broadcasted_iota(jnp.int32, cos_ref.shape, 0) + pl.program_id(0) * tt).astype(F32)
    ang = pos * inv_ref[...]
    cos_ref[...] = jnp.cos(ang)
    sin_ref[...] = jnp.sin(ang) * sign_ref[...]


def _rope_tables(t, head_dim, tt=512):
    tt = min(tt, t)
    half = head_dim // 2
    inv_freq = jnp.power(ROPE_BASE, -jnp.linspace(0.0, 1.0, half, dtype=F32))
    inv_pair = jnp.repeat(inv_freq, 2).reshape(1, head_dim)
    sign = jnp.tile(jnp.array([-1.0, 1.0], F32), half).reshape(1, head_dim)
    row = pl.BlockSpec((1, head_dim), lambda i: (0, 0))
    tab = pl.BlockSpec((tt, head_dim), lambda i: (i, 0))
    return pl.pallas_call(
        functools.partial(_rope_kernel, tt=tt),
        out_shape=(jax.ShapeDtypeStruct((t, head_dim), F32), jax.ShapeDtypeStruct((t, head_dim), F32)),
        grid=(t // tt,),
        in_specs=[row, row],
        out_specs=(tab, tab),
        compiler_params=_params(("parallel",)),
        name="rope_tables",
    )(inv_pair, sign)


def _gates_kernel(x_ref, alog_ref, dtb_ref, o_ref, *, n_heads, chunk):
    x = x_ref[...]
    beta = _sigmoid(x)
    g = -jnp.exp(alog_ref[...]) * _softplus(x + dtb_ref[...])
    gc = _split_dot(_chunk_cumsum_mask(x.shape[0], chunk), g, 3)
    lane = lax.broadcasted_iota(jnp.int32, x.shape, 1)
    o_ref[...] = jnp.where(lane < n_heads, beta, gc)


def _gates(ba, a_log, dt_bias, n_heads, chunk, tg=256):
    m, w = ba.shape
    tg = min(tg, m)
    alog_p = jnp.zeros((1, w), F32).at[0, n_heads:2 * n_heads].set(a_log.astype(F32))
    dtb_p = jnp.zeros((1, w), F32).at[0, n_heads:2 * n_heads].set(dt_bias.astype(F32))
    return pl.pallas_call(
        functools.partial(_gates_kernel, n_heads=n_heads, chunk=chunk),
        out_shape=jax.ShapeDtypeStruct((m, w), F32),
        grid=(m // tg,),
        in_specs=[pl.BlockSpec((tg, w), lambda i: (i, 0)),
                  pl.BlockSpec((1, w), lambda i: (0, 0)),
                  pl.BlockSpec((1, w), lambda i: (0, 0))],
        out_specs=pl.BlockSpec((tg, w), lambda i: (i, 0)),
        compiler_params=_params(("parallel",)),
        name="deltanet_gates",
    )(ba, alog_p, dtb_p)


def _unit_lower_inverses(mats, n):
    ii = lax.broadcasted_iota(jnp.int32, (n, n), 0)
    jj = lax.broadcasted_iota(jnp.int32, (n, n), 1)
    base = 16
    eye = jnp.where(ii == jj, 1.0, 0.0)
    diag_blocks = (ii // base) == (jj // base)
    ps = [jnp.where(diag_blocks, -a, 0.0) for a in mats]
    ts = [eye + p for p in ps]
    width = 2
    while width < base:
        ps = [_dot(p, p) for p in ps]
        ts = [t + _dot(t, p) for t, p in zip(ts, ps)]
        width *= 2
    bs = base
    while bs < n:
        off_blocks = ((ii // (2 * bs)) == (jj // (2 * bs))) & ((ii // bs) != (jj // bs))
        xs = [_dot(t, jnp.where(off_blocks, a, 0.0)) for t, a in zip(ts, mats)]
        ts = [t - _dot(x, t) for t, x in zip(ts, xs)]
        bs *= 2
    return ts


def _mixer_a_kernel(q_ref, k_ref, v_ref, z_ref, gate_ref, gct_ref, na_ref, o_ref, s_ref, *, tb, n_heads):
    d = A_HEAD_DIM
    heads = range(n_heads)

    @pl.when(pl.program_id(1) == 0)
    def _init():
        s_ref[...] = jnp.zeros_like(s_ref)

    q_all, k_all, v_all, z_all = q_ref[0], k_ref[0], v_ref[0], z_ref[0]
    gt = gate_ref[0]
    lane = lax.broadcasted_iota(jnp.int32, gt.shape, 1)

    ii = lax.broadcasted_iota(jnp.int32, (tb, tb), 0)
    jj = lax.broadcasted_iota(jnp.int32, (tb, tb), 1)
    causal = ii >= jj
    strict = ii > jj
    gain = na_ref[...]

    def head_cols(x, h):
        return x[:, h * d:(h + 1) * d]

    q = [head_cols(q_all, h) for h in heads]
    k = [head_cols(k_all, h) for h in heads]
    beta =[jnp.sum(jnp.where(lane == h, gt, 0.0), axis=1, keepdims=True) for h in heads]
    gc = [jnp.sum(jnp.where(lane == h + n_heads, gt, 0.0), axis=1, keepdims=True) for h in heads]
    grow = gct_ref[0]
    decay = [jnp.where(causal, jnp.exp(jnp.where(causal, gc[h] - grow[h:h + 1], 0.0)), 0.0) for h in heads]
    kb = [k[h] * beta[h] for h in heads]
    eg = [jnp.exp(gc[h]) for h in heads]

    kq = [_dot_nt(jnp.concatenate([kb[h], q[h]], axis=0), k[h]) for h in heads]
    a = [jnp.where(strict, kq[h][:tb] * decay[h], 0.0) for h in heads]
    scores = [kq[h][tb:] * decay[h] for h in heads]
    t = _unit_lower_inverses(a, tb)
    uw = [_dot(t[h], jnp.concatenate([head_cols(v_all, h) * beta[h], kb[h] * eg[h]], axis=1)) for h in heads]
    s = [s_ref[h] for h in heads]
    ws = [_dot(jnp.concatenate([uw[h][:, d:], q[h] * eg[h]], axis=0), s[h]) for h in heads]
    v_new = [uw[h][:, :d] - ws[h][:tb] for h in heads]
    o = [ws[h][tb:] + _dot(scores[h], v_new[h]) for h in heads]
    g_last = [gc[h][tb - 1:tb] for h in heads]
    kv = [_dot_tn(k[h] * jnp.exp(g_last[h] - gc[h]), v_new[h]) for h in heads]
    for h in heads:
        s_ref[h] = s[h] * jnp.exp(g_last[h]) + kv[h]
        on = o[h] * lax.rsqrt(jnp.mean(o[h] * o[h], axis=-1, keepdims=True) + NORM_EPS) * gain
        o_ref[0, :, h * d:(h + 1) * d] = (on * head_cols(z_all, h)).astype(o_ref.dtype)


def _mixer_a(q, k, v, z, gates, gates_t, norm_a, n_heads):
    b, t, hd = q.shape
    d = A_HEAD_DIM
    tb = min(A_CHUNK, t)
    hh = n_heads
    col = pl.BlockSpec((1, tb, hd), lambda bi, ti: (bi, ti, 0))
    return pl.pallas_call(
        functools.partial(_mixer_a_kernel, tb=tb, n_heads=hh),
        out_shape=jax.ShapeDtypeStruct((b, t, hd), BF16),
        grid=(b, t // tb),
        in_specs=[col, col, col, col,
                  pl.BlockSpec((1, tb, gates.shape[-1]), lambda bi, ti: (bi, ti, 0)),
                  pl.BlockSpec((1, hh, tb), lambda bi, ti: (bi, 0, ti)),
                  pl.BlockSpec((1, d), lambda bi, ti: (0, 0))],
        out_specs=col,
        scratch_shapes=[pltpu.VMEM((hh, d, d), F32)],
        compiler_params=_params(("parallel", "arbitrary")),
        name="mixer_deltanet",
    )(q, k, v, z, gates, gates_t, norm_a.reshape(1, d).astype(F32))


def _mixer_b_kernel(q_ref, k_ref, v_ref, g_ref, nb_ref, o_ref, s_ref, decay_ref, *, chunk, n_heads):
    d = B_HEAD_DIM
    heads = range(n_heads)
    log_gamma = [math.log1p(-(2.0 ** (-5.0 - h))) for h in heads]

    @pl.when(pl.program_id(1) == 0)
    def _init():
        s_ref[...] = jnp.zeros_like(s_ref)
        ii = lax.broadcasted_iota(jnp.int32, (chunk, chunk), 0)
        jj = lax.broadcasted_iota(jnp.int32, (chunk, chunk), 1)
        causal = ii >= jj
        dist = jnp.where(causal, ii - jj, 0).astype(F32)
        for h in heads:
            decay_ref[h] = jnp.where(causal, jnp.exp(dist * log_gamma[h]), 0.0)

    pos = lax.broadcasted_iota(jnp.int32, (chunk, 1), 0).astype(F32)
    q_all, k_all, v_all, g_all = q_ref[0], k_ref[0], v_ref[0], g_ref[0]

    def head_cols(x, h):
        return x[:, h * d:(h + 1) * d]

    q = [head_cols(q_all, h) for h in heads]
    k = [head_cols(k_all, h) for h in heads]
    v = [head_cols(v_all, h) for h in heads]
    s = [s_ref[h] for h in heads]
    scores = [_dot_nt(q[h], k[h]) * decay_ref[h] for h in heads]
    cross = [_dot(q[h] * jnp.exp((pos + 1.0) * log_gamma[h]), s[h]) for h in heads]
    o = [_dot(scores[h], v[h]) + cross[h] for h in heads]
    kv = [_dot_tn(k[h] * jnp.exp((chunk - 1.0 - pos) * log_gamma[h]), v[h]) for h in heads]
    gain = nb_ref[...]
    for h in heads:
        s_ref[h] = s[h] * math.exp(chunk * log_gamma[h]) + kv[h]
        on = o[h] * lax.rsqrt(jnp.mean(o[h] * o[h], axis=-1, keepdims=True) + NORM_EPS) * gain
        o_ref[0, :, h * d:(h + 1) * d] = (on * head_cols(g_all, h)).astype(o_ref.dtype)


def _mixer_b(q, k, v, g, norm_b, n_heads):
    b, t, hd = q.shape
    d = B_HEAD_DIM
    chunk = min(B_CHUNK, t)
    hh = n_heads
    col = pl.BlockSpec((1, chunk, hd), lambda bi, ti: (bi, ti, 0))
    return pl.pallas_call(
        functools.partial(_mixer_b_kernel, chunk=chunk, n_heads=hh),
        out_shape=jax.ShapeDtypeStruct((b, t, hd), BF16),
        grid=(b, t // chunk),
        in_specs=[col, col, col, col, pl.BlockSpec((1, d), lambda bi, ti: (0, 0))],
        out_specs=col,
        scratch_shapes=[pltpu.VMEM((hh, d, d), F32), pltpu.VMEM((hh, chunk, chunk), F32)],
        compiler_params=_params(("arbitrary", "arbitrary")),
        name="mixer_retention",
    )(q, k, v, g, norm_b.reshape(1, d).astype(F32))


def _mixer_c_kernel(q_ref, k_ref, v_ref, r_ref, lr_ref, wup_ref, bgk_ref, nc_ref, o_ref, st_ref, *, chunk, n_heads):
    heads = range(n_heads)

    @pl.when(pl.program_id(1) == 0)
    def _init():
        st_ref[...] = jnp.zeros_like(st_ref)

    dk = q_ref.shape[-1] // n_heads
    dv = v_ref.shape[-1] // n_heads
    q_all, k_all, v_all, r_all = q_ref[0], k_ref[0], v_ref[0], r_ref[0]

    logit = _dot(lr_ref[0], wup_ref[...]) + bgk_ref[...]
    gk = -_softplus(-logit) / GK_NORMALIZER
    gc_all = _split_dot(_chunk_cumsum_mask(chunk, chunk), gk, 2)
    sub = min(C_SUBCHUNK, chunk)
    ii = lax.broadcasted_iota(jnp.int32, (sub, sub), 0)
    jj = lax.broadcasted_iota(jnp.int32, (sub, sub), 1)
    causal = ii >= jj

    q = [q_all[:, h * dk:(h + 1) * dk] for h in heads]
    k = [k_all[:, h * dk:(h + 1) * dk] for h in heads]
    v = [v_all[:, h * dv:(h + 1) * dv] for h in heads]
    gc = [gc_all[:, h * dk:(h + 1) * dk] for h in heads]
    g_last = [gc[h][chunk - 1:chunk] for h in heads]
    st = [st_ref[h] for h in heads]

    def score_rows(h, r0):
        rows = slice(r0, r0 + sub)
        mid = gc[h][r0 + sub // 2 - 1:r0 + sub // 2]
        diag = jnp.where(causal, _dot_nt(q[h][rows] * jnp.exp(gc[h][rows] - mid),
                                         k[h][rows] * jnp.exp(mid - gc[h][rows])), 0.0)
        parts = [diag]
        if r0 > 0:
            bnd = gc[h][r0 - 1:r0]
            parts.insert(0, _dot_nt(q[h][rows] * jnp.exp(gc[h][rows] - bnd), k[h][:r0] * jnp.exp(bnd - gc[h][:r0])))
        if r0 + sub < chunk:
            parts.append(jnp.zeros((sub, chunk - r0 - sub), F32))
        return jnp.concatenate(parts, axis=1) if len(parts) > 1 else diag

    score_blocks = [[score_rows(h, r0) for h in heads] for r0 in range(0, chunk, sub)]
    scores = [jnp.concatenate([blk[h] for blk in score_blocks], axis=0) if len(score_blocks) > 1
              else score_blocks[0][h] for h in heads]
    cross = [_dot_nt(q[h] * jnp.exp(gc[h]), st[h]) for h in heads]
    o = [_dot(scores[h], v[h]) + cross[h] for h in heads]
    kv = [_dot_tn(v[h], k[h] * jnp.exp(g_last[h] - gc[h])) for h in heads]
    gain = nc_ref[...]
    for h in heads:
        st_ref[h] = st[h] * jnp.exp(g_last[h]) + kv[h]
        on = o[h] * lax.rsqrt(jnp.mean(o[h] * o[h], axis=-1, keepdims=True) + NORM_EPS) * gain
        o_ref[0, :, h * dv:(h + 1) * dv] = (on * r_all[:, h * dv:(h + 1) * dv]).astype(o_ref.dtype)


def _mixer_c(q, k, v, r, lr, w_up, b_gk, norm_c):
    b, t, kw = q.shape
    vw = v.shape[-1]
    hh = C_HEADS
    dk, dv = kw // hh, vw // hh
    chunk = min(C_CHUNK, t)
    rw = lr.shape[-1]
    kcol = pl.BlockSpec((1, chunk, kw), lambda bi, ti: (bi, ti, 0))
    vcol = pl.BlockSpec((1, chunk, vw), lambda bi, ti: (bi, ti, 0))
    return pl.pallas_call(
        functools.partial(_mixer_c_kernel, chunk=chunk, n_heads=hh),
        out_shape=jax.ShapeDtypeStruct((b, t, vw), BF16),
        grid=(b, t // chunk),
        in_specs=[kcol, kcol, vcol, vcol,
                  pl.BlockSpec((1, chunk, rw), lambda bi, ti: (bi, ti, 0)),
                  pl.BlockSpec((rw, kw), lambda bi, ti: (0, 0)),
                  pl.BlockSpec((1, kw), lambda bi, ti: (0, 0)),
                  pl.BlockSpec((1, dv), lambda bi, ti: (0, 0))],
        out_specs=vcol,
        scratch_shapes=[pltpu.VMEM((hh, dv, dk), F32)],
        compiler_params=_params(("parallel", "arbitrary")),
        name="mixer_gla",
    )(q, k, v, r, lr, w_up, b_gk.reshape(1, kw).astype(F32), norm_c.reshape(1, dv).astype(F32))


def _pad_cols(w, width):
    return jnp.pad(w, ((0, 0), (0, width - w.shape[1])))


def _mlp(x2d, xb, ssq, w_up_g, w_down, layer, emit_stats):
    hid = _matmul(xb, w_up_g, layer=layer, row_ssq=ssq, act="relu2", out_dtype=BF16)
    return _matmul(hid, w_down, layer=layer, res=x2d, tk=4096, emit_stats=emit_stats)


def _even_layer(x2d, b, t, gain, w_in, conv_w, a_log, dt_bias, norm_a, norm_b, w_out):
    d = x2d.shape[1]
    ha, hb = d // 256, d // 512
    akw = ha * A_HEAD_DIM
    bkw = hb * B_HEAD_DIM
    small0 = 4 * akw
    b0 = small0 + 2 * ha
    conv_w = conv_w.astype(F32)

    def seq(a2d):
        return a2d.reshape(b, t, -1)

    hn = _rmsnorm(x2d, gain, BF16)
    w_bf = w_in.astype(BF16)

    def w_slice(lo, width):
        return w_bf[:, lo:lo + width]

    q_a = seq(_conv_proj(hn, w_bf, conv_w, (0, akw), t, norm_dim=A_HEAD_DIM, scale=A_HEAD_DIM ** -0.5))
    k_a = seq(_conv_proj(hn, w_bf, conv_w, (akw, akw), t, norm_dim=A_HEAD_DIM))
    v_a = seq(_conv_proj(hn, w_bf, conv_w, (2 * akw, akw), t))
    z_a = seq(_matmul(hn, w_bf, cols=(3 * akw, akw), act="silu"))
    ba = _matmul(hn, _pad_cols(w_slice(small0, 2 * ha), LANES))
    gates = _gates(ba, a_log, dt_bias, ha, A_CHUNK).reshape(b, t, LANES)
    gates_t = jnp.swapaxes(gates[:, :, ha:2 * ha], 1, 2)
    o_a = _mixer_a(q_a, k_a, v_a, z_a, gates, gates_t, norm_a, ha)
    cos, sin = _rope_tables(t, B_HEAD_DIM)
    q_b = seq(_rope_proj(hn, w_slice(b0, bkw), cos, sin, t, B_HEAD_DIM))
    k_b = seq(_rope_proj(hn, w_slice(b0 + bkw, bkw), cos, sin, t, B_HEAD_DIM, scale=B_HEAD_DIM ** -0.5))
    v_b = seq(_matmul(hn, w_slice(b0 + 2 * bkw, bkw)))
    g_b = seq(_matmul(hn, w_slice(b0 + 3 * bkw, bkw), act="silu"))
    o_b = _mixer_b(q_b, k_b, v_b, g_b, norm_b, hb)
    return _matmul(o_a.reshape(b * t, akw), w_out.astype(BF16), a2=o_b.reshape(b * t, bkw), res=x2d,
                   emit_stats=True)


def _odd_layer(x2d, xb, ssq, b, t, gain, w_in, w_gk_down, w_gk_up, b_gk, norm_c, w_out):
    d = x2d.shape[1]
    kw, vw = d // 2, d
    dk = kw // C_HEADS

    def seq(a2d):
        return a2d.reshape(b, t, -1)

    gcol = gain.astype(F32)[:, None]
    w_bf = (w_in * gcol).astype(BF16)
    q = seq(_matmul(xb, w_bf, cols=(0, kw), row_ssq=ssq, act=dk ** -0.5))
    k = seq(_matmul(xb, w_bf, cols=(kw, kw), row_ssq=ssq))
    v = seq(_matmul(xb, w_bf, cols=(2 * kw, vw), row_ssq=ssq))
    r = seq(_matmul(xb, w_bf, cols=(2 * kw + vw, vw), row_ssq=ssq, act="silu"))
    lr = seq(_matmul(xb, _pad_cols(w_gk_down * gcol, LANES).astype(BF16), row_ssq=ssq))
    w_up = jnp.pad(w_gk_up, ((0, LANES - w_gk_up.shape[0]), (0, 0))).astype(BF16)
    o_c = _mixer_c(q, k, v, r, lr, w_up, b_gk, norm_c).reshape(b * t, vw)
    return _matmul(o_c, w_out.astype(BF16), res=x2d, emit_stats=True)


def kernel(x, norm_mix, norm_mlp, norm_final, w_up, w_down, w_in_ab, conv_a, a_log, dt_bias, norm_a, norm_b,
           w_out_ab, w_in_c, w_gk_down, w_gk_up, b_gk, norm_c, w_out_c):
    b, t, d = x.shape
    depth = norm_mix.shape[0]
    x2d = x.reshape(b * t, d)
    w_up_g = (w_up * norm_mlp.astype(F32)[:, :, None]).astype(BF16)
    w_down = w_down.astype(BF16)
    xb = ssq = None
    for layer in range(depth):
        i = layer // 2
        if layer % 2 == 0:
            x2d, xb, ssq = _even_layer(x2d, b, t, norm_mix[layer], w_in_ab[i], conv_a[i], a_log[i], dt_bias[i],
                                       norm_a[i], norm_b[i], w_out_ab[i])
        else:
            x2d, xb, ssq = _odd_layer(x2d, xb, ssq, b, t, norm_mix[layer], w_in_c[i], w_gk_down[i], w_gk_up[i],
                                      b_gk[i], norm_c[i], w_out_c[i])
        if layer + 1 < depth:
            x2d, xb, ssq = _mlp(x2d, xb, ssq, w_up_g, w_down, layer, emit_stats=True)
        else:
            x2d = _mlp(x2d, xb, ssq, w_up_g, w_down, layer, emit_stats=False)
    return _rmsnorm(x2d, norm_final, F32).reshape(b, t, d)
```

```python
import functools
import math

import jax
import jax.numpy as jnp
from jax import lax
from jax.experimental import pallas as pl
from jax.experimental.pallas import tpu as pltpu

F32 = jnp.float32
BF16 = jnp.bfloat16

NORM_EPS = 1e-6
ROPE_BASE = 10000.0
CONV_K = 4
GK_NORMALIZER = 16.0
A_HEAD_DIM = 128
B_HEAD_DIM = 256
C_HEADS = 4
LANES = 128
SUBLANES = 8
MXU_COLS = 256
CONV_PROJ_ROWS = 512
A_CHUNK = 128
B_CHUNK = 256
C_CHUNK = 256
C_SUBCHUNK = 128
MIB = 1024 * 1024
VMEM_MIB_WITH_STATS = 62


def _params(semantics, vmem_mib=None):
    kwargs = dict(dimension_semantics=semantics)
    if vmem_mib is not None:
        kwargs["vmem_limit_bytes"] = vmem_mib * MIB
    return pltpu.CompilerParams(**kwargs)


def _dot(a, b):
    return jnp.dot(a.astype(BF16), b.astype(BF16), preferred_element_type=F32)


def _dot_nt(a, b):
    return lax.dot_general(a.astype(BF16), b.astype(BF16), (((1,), (1,)), ((), ())),
                           preferred_element_type=F32)


def _dot_tn(a, b):
    return lax.dot_general(a.astype(BF16), b.astype(BF16), (((0,), (0,)), ((), ())),
                           preferred_element_type=F32)


def _sigmoid(x):
    return 1.0 / (1.0 + jnp.exp(-x))


def _silu(x):
    return x * _sigmoid(x)


def _softplus(x):
    return jnp.maximum(x, 0.0) + jnp.log1p(jnp.exp(-jnp.abs(x)))


def _split_dot(mask_bf16, g, pieces):
    acc = None
    rem = g
    for _ in range(pieces):
        part = rem.astype(BF16)
        term = jnp.dot(mask_bf16, part, preferred_element_type=F32)
        acc = term if acc is None else acc + term
        rem = rem - part.astype(F32)
    return acc


def _chunk_cumsum_mask(n, chunk):
    ii = lax.broadcasted_iota(jnp.int32, (n, n), 0)
    jj = lax.broadcasted_iota(jnp.int32, (n, n), 1)
    same = (ii // chunk) == (jj // chunk)
    return jnp.where(same & (ii >= jj), 1.0, 0.0).astype(BF16)


def _rmsnorm_kernel(x_ref, g_ref, o_ref):
    x = x_ref[...]
    ms = jnp.mean(x * x, axis=-1, keepdims=True)
    o_ref[...] = (x * lax.rsqrt(ms + NORM_EPS) * g_ref[...]).astype(o_ref.dtype)


def _rmsnorm(x2d, gain, out_dtype, tm=512):
    m, d = x2d.shape
    tm = min(tm, m)
    return pl.pallas_call(
        _rmsnorm_kernel,
        out_shape=jax.ShapeDtypeStruct((m, d), out_dtype),
        grid=(m // tm,),
        in_specs=[pl.BlockSpec((tm, d), lambda i: (i, 0)),
                  pl.BlockSpec((1, d), lambda i: (0, 0))],
        out_specs=pl.BlockSpec((tm, d), lambda i: (i, 0)),
        compiler_params=_params(("parallel",)),
        name="rmsnorm",
    )(x2d, gain.reshape(1, d).astype(F32))


def _matmul_kernel(*refs, nk, n_pairs, act, has_res, has_scale, emit_stats, norm_dim):
    ab_refs = refs[:2 * n_pairs]
    pos = 2 * n_pairs
    ssq_in_ref = refs[pos] if has_scale else None
    pos += int(has_scale)
    res_ref = refs[pos] if has_res else None
    pos += int(has_res)
    o_ref = refs[pos]
    xb_ref, ssq_out_ref = (refs[pos + 1], refs[pos + 2]) if emit_stats else (None, None)
    j = pl.program_id(1)

    def product():
        acc = None
        for p in range(n_pairs):
            term = jnp.dot(ab_refs[2 * p][...], ab_refs[2 * p + 1][...], preferred_element_type=F32)
            acc = term if acc is None else acc + term
        return acc

    def stats(x_new):
        xb_ref[...] = x_new.astype(xb_ref.dtype)
        part = jnp.broadcast_to(jnp.sum(x_new * x_new, axis=-1, keepdims=True), ssq_out_ref.shape)
        ssq_out_ref[...] = jnp.where(j == 0, part, ssq_out_ref[...] + part)

    if nk == 1:
        acc = product()
        if has_scale:
            acc = acc * lax.rsqrt(ssq_in_ref[:, 0:1] * (1.0 / norm_dim) + NORM_EPS)
        if act == "relu2":
            r = jnp.maximum(acc, 0.0)
            acc = r * r
        elif act == "silu":
            acc = _silu(acc)
        elif isinstance(act, float):
            acc = acc * act
        if has_res:
            acc = res_ref[...] + acc
        o_ref[...] = acc.astype(o_ref.dtype)
        if emit_stats:
            stats(acc)
    else:
        k = pl.program_id(2)

        @pl.when(k == 0)
        def _first():
            o_ref[...] = res_ref[...] if has_res else jnp.zeros_like(o_ref)

        o_ref[...] += product()
        if emit_stats:
            @pl.when(k == nk - 1)
            def _last():
                stats(o_ref[...])


def _matmul(a, b, *, layer=None, cols=None, a2=None, row_ssq=None, res=None, act=None, emit_stats=False,
            out_dtype=F32, tm=1024, tn=1024, tk=None, vmem_mib=56):
    m, ka = a.shape
    col0, n = (0, b.shape[-1]) if cols is None else cols
    n_pairs = 1 if a2 is None else 2
    tm, tn = min(tm, m), min(tn, n)
    tk = ka if tk is None else min(tk, ka)
    assert m % tm == 0 and n % tn == 0 and ka % tk == 0 and col0 % tn == 0
    jb0 = col0 // tn
    nk = ka // tk
    has_res = res is not None
    has_scale = row_ssq is not None
    assert nk == 1 or (act is None and out_dtype == F32 and a2 is None and not has_scale)
    stacked = layer is not None

    def a_map(i, j, *k):
        return (i, k[0] if k else 0)

    def o_map(i, j, *k):
        return (i, j)

    def row_map(i, j, *k):
        return (i, 0)

    def b_spec(row_block):
        def b_map(i, j, *k):
            kb = (k[0] if k else 0) + row_block
            return (layer, kb, j + jb0) if stacked else (kb, j + jb0)
        return pl.BlockSpec((None, tk, tn) if stacked else (tk, tn), b_map)

    a_spec = pl.BlockSpec((tm, tk), a_map)
    o_spec = pl.BlockSpec((tm, tn), o_map)
    ssq_spec = pl.BlockSpec((tm, LANES), row_map)
    in_specs = [a_spec, b_spec(0)]
    args = [a, b]
    if a2 is not None:
        in_specs += [a_spec, b_spec(1)]
        args += [a2, b]
    if has_scale:
        in_specs.append(ssq_spec)
        args.append(row_ssq)
    if has_res:
        in_specs.append(o_spec)
        args.append(res)
    out_shape = jax.ShapeDtypeStruct((m, n), out_dtype)
    out_specs = o_spec
    if emit_stats:
        out_shape = (out_shape, jax.ShapeDtypeStruct((m, n), BF16), jax.ShapeDtypeStruct((m, LANES), F32))
        out_specs = (o_spec, o_spec, ssq_spec)
    if emit_stats:
        vmem_mib = max(vmem_mib, VMEM_MIB_WITH_STATS)
    grid = (m // tm, n // tn) + ((nk,) if nk > 1 else ())
    sem = ("parallel", "arbitrary" if emit_stats else "parallel") + (("arbitrary",) if nk > 1 else ())
    return pl.pallas_call(
        functools.partial(_matmul_kernel, nk=nk, n_pairs=n_pairs, act=act, has_res=has_res, has_scale=has_scale,
                          emit_stats=emit_stats, norm_dim=ka),
        out_shape=out_shape,
        grid=grid,
        in_specs=in_specs,
        out_specs=out_specs,
        compiler_params=_params(sem, vmem_mib),
        name="matmul",
    )(*args)


def _conv_proj_kernel(a_ref, b_ref, cw_ref, o_ref, carry_ref, raw_ref, *, tiles_per_seq, norm_dim, scale):
    i, j = pl.program_id(0), pl.program_id(1)
    tm, tn = o_ref.shape
    rows_per_dot = min(CONV_PROJ_ROWS, tm)
    first_tile = lax.rem(i, tiles_per_seq) == 0

    prev = carry_ref[j]
    raw_ref[0:SUBLANES, :] = jnp.where(first_tile, jnp.zeros_like(prev), prev)

    def epilogue(r0, c0):
        cols = slice(c0, c0 + MXU_COLS)
        if r0 + rows_per_dot == tm:
            carry_ref[j, :, cols] = raw_ref[tm:tm + SUBLANES, cols]
        w = cw_ref[:, cols]
        y = w[CONV_K - 1:CONV_K] * raw_ref[SUBLANES + r0:SUBLANES + r0 + rows_per_dot, cols]
        for tap in range(CONV_K - 1):
            off = SUBLANES - (CONV_K - 1) + tap + r0
            y = y + w[tap:tap + 1] * raw_ref[off:off + rows_per_dot, cols]
        y = _silu(y)
        if norm_dim is not None:
            segs = []
            for s0 in range(0, MXU_COLS, norm_dim):
                seg = y[:, s0:s0 + norm_dim]
                segs.append(seg * (lax.rsqrt(jnp.sum(seg * seg, axis=-1, keepdims=True) + NORM_EPS) * scale))
            y = jnp.concatenate(segs, axis=1)
        o_ref[r0:r0 + rows_per_dot, cols] = y.astype(o_ref.dtype)

    subtiles = [(r0, c0) for r0 in range(0, tm, rows_per_dot) for c0 in range(0, tn, MXU_COLS)]
    pending = None
    for r0, c0 in subtiles:
        raw_ref[SUBLANES + r0:SUBLANES + r0 + rows_per_dot, c0:c0 + MXU_COLS] = jnp.dot(
            a_ref[r0:r0 + rows_per_dot, :], b_ref[:, c0:c0 + MXU_COLS], preferred_element_type=F32)
        if pending is not None:
            epilogue(*pending)
        pending = (r0, c0)
    epilogue(*pending)


def _conv_proj(a, b, conv_w, cols, seq_len, *, norm_dim=None, scale=1.0, tm=1024, tn=1024, vmem_mib=56):
    m, kdim = a.shape
    col0, n = cols
    tm, tn = min(tm, seq_len), min(tn, n)
    assert m % tm == 0 and n % tn == 0 and seq_len % tm == 0 and col0 % tn == 0
    jb0 = col0 // tn
    return pl.pallas_call(
        functools.partial(_conv_proj_kernel, tiles_per_seq=seq_len // tm, norm_dim=norm_dim, scale=scale),
        out_shape=jax.ShapeDtypeStruct((m, n), F32),
        grid=(m // tm, n // tn),
        in_specs=[pl.BlockSpec((tm, kdim), lambda i, j: (i, 0)),
                  pl.BlockSpec((kdim, tn), lambda i, j: (0, j + jb0)),
                  pl.BlockSpec((CONV_K, tn), lambda i, j: (0, j + jb0))],
        out_specs=pl.BlockSpec((tm, tn), lambda i, j: (i, j)),
        scratch_shapes=[pltpu.VMEM((n // tn, SUBLANES, tn), F32), pltpu.VMEM((tm + SUBLANES, tn), F32)],
        compiler_params=_params(("arbitrary", "arbitrary"), vmem_mib),
        name="conv_proj",
    )(a, b, conv_w)


def _rope_proj_kernel(a_ref, b_ref, cos_ref, sin_ref, o_ref, *, head_dim, scale):
    acc = jnp.dot(a_ref[...], b_ref[...], preferred_element_type=F32)
    tm, tn = acc.shape
    even = (lax.broadcasted_iota(jnp.int32, (tm, LANES), 1) % 2) == 0
    parts = []
    for c0 in range(0, tn, LANES):
        x = acc[:, c0:c0 + LANES]
        partner = jnp.where(even, pltpu.roll(x, LANES - 1, axis=1), pltpu.roll(x, 1, axis=1))
        t0 = c0 % head_dim
        parts.append((x * cos_ref[:, t0:t0 + LANES] + partner * sin_ref[:, t0:t0 + LANES]) * scale)
    o_ref[...] = jnp.concatenate(parts, axis=1).astype(o_ref.dtype)


def _rope_proj(a, b, cos, sin, seq_len, head_dim, *, scale=1.0, tm=1024, tn=1024, vmem_mib=56):
    m, kdim = a.shape
    n = b.shape[1]
    tm, tn = min(tm, seq_len), min(tn, n)
    assert m % tm == 0 and n % tn == 0 and seq_len % tm == 0 and tn % head_dim == 0
    tiles_per_seq = seq_len // tm
    tab = pl.BlockSpec((tm, head_dim), lambda i, j: (i % tiles_per_seq, 0))
    return pl.pallas_call(
        functools.partial(_rope_proj_kernel, head_dim=head_dim, scale=scale),
        out_shape=jax.ShapeDtypeStruct((m, n), F32),
        grid=(m // tm, n // tn),
        in_specs=[pl.BlockSpec((tm, kdim), lambda i, j: (i, 0)),
                  pl.BlockSpec((kdim, tn), lambda i, j: (0, j)),
                  tab, tab],
        out_specs=pl.BlockSpec((tm, tn), lambda i, j: (i, j)),
        compiler_params=_params(("parallel", "parallel"), vmem_mib),
        name="rope_proj",
    )(a, b, cos, sin)


def _rope_kernel(inv_ref, sign_ref, cos_ref, sin_ref, *, tt):
    pos = (lax.broadcasted_iota(jnp.int32, cos_ref.shape, 0) + pl.program_id(0) * tt).astype(F32)
    ang = pos * inv_ref[...]
    cos_ref[...] = jnp.cos(ang)
    sin_ref[...] = jnp.sin(ang) * sign_ref[...]


def _rope_tables(t, head_dim, tt=512):
    tt = min(tt, t)
    half = head_dim // 2
    inv_freq = jnp.power(ROPE_BASE, -jnp.linspace(0.0, 1.0, half, dtype=F32))
    inv_pair = jnp.repeat(inv_freq, 2).reshape(1, head_dim)
    sign = jnp.tile(jnp.array([-1.0, 1.0], F32), half).reshape(1, head_dim)
    row = pl.BlockSpec((1, head_dim), lambda i: (0, 0))
    tab = pl.BlockSpec((tt, head_dim), lambda i: (i, 0))
    return pl.pallas_call(
        functools.partial(_rope_kernel, tt=tt),
        out_shape=(jax.ShapeDtypeStruct((t, head_dim), F32), jax.ShapeDtypeStruct((t, head_dim), F32)),
        grid=(t // tt,),
        in_specs=[row, row],
        out_specs=(tab, tab),
        compiler_params=_params(("parallel",)),
        name="rope_tables",
    )(inv_pair, sign)


def _gates_kernel(x_ref, alog_ref, dtb_ref, o_ref, *, n_heads, chunk):
    x = x_ref[...]
    beta = _sigmoid(x)
    g = -jnp.exp(alog_ref[...]) * _softplus(x + dtb_ref[...])
    gc = _split_dot(_chunk_cumsum_mask(x.shape[0], chunk), g, 3)
    lane = lax.broadcasted_iota(jnp.int32, x.shape, 1)
    o_ref[...] = jnp.where(lane < n_heads, beta, gc)


def _gates(ba, a_log, dt_bias, n_heads, chunk, tg=256):
    m, w = ba.shape
    tg = min(tg, m)
    alog_p = jnp.zeros((1, w), F32).at[0, n_heads:2 * n_heads].set(a_log.astype(F32))
    dtb_p = jnp.zeros((1, w), F32).at[0, n_heads:2 * n_heads].set(dt_bias.astype(F32))
    return pl.pallas_call(
        functools.partial(_gates_kernel, n_heads=n_heads, chunk=chunk),
        out_shape=jax.ShapeDtypeStruct((m, w), F32),
        grid=(m // tg,),
        in_specs=[pl.BlockSpec((tg, w), lambda i: (i, 0)),
                  pl.BlockSpec((1, w), lambda i: (0, 0)),
                  pl.BlockSpec((1, w), lambda i: (0, 0))],
        out_specs=pl.BlockSpec((tg, w), lambda i: (i, 0)),
        compiler_params=_params(("parallel",)),
        name="deltanet_gates",
    )(ba, alog_p, dtb_p)


def _unit_lower_inverses(mats, n):
    ii = lax.broadcasted_iota(jnp.int32, (n, n), 0)
    jj = lax.broadcasted_iota(jnp.int32, (n, n), 1)
    base = 16
    eye = jnp.where(ii == jj, 1.0, 0.0)
    diag_blocks = (ii // base) == (jj // base)
    ps = [jnp.where(diag_blocks, -a, 0.0) for a in mats]
    ts = [eye + p for p in ps]
    width = 2
    while width < base:
        ps = [_dot(p, p) for p in ps]
        ts = [t + _dot(t, p) for t, p in zip(ts, ps)]
        width *= 2
    bs = base
    while bs < n:
        off_blocks = ((ii // (2 * bs)) == (jj // (2 * bs))) & ((ii // bs) != (jj // bs))
        xs = [_dot(t, jnp.where(off_blocks, a, 0.0)) for t, a in zip(ts, mats)]
        ts = [t - _dot(x, t) for t, x in zip(ts, xs)]
        bs *= 2
    return ts


def _mixer_a_kernel(q_ref, k_ref, v_ref, z_ref, gate_ref, gct_ref, na_ref, o_ref, s_ref, *, tb, n_heads):
    d = A_HEAD_DIM
    heads = range(n_heads)

    @pl.when(pl.program_id(1) == 0)
    def _init():
        s_ref[...] = jnp.zeros_like(s_ref)

    q_all, k_all, v_all, z_all = q_ref[0], k_ref[0], v_ref[0], z_ref[0]
    gt = gate_ref[0]
    lane = lax.broadcasted_iota(jnp.int32, gt.shape, 1)

    ii = lax.broadcasted_iota(jnp.int32, (tb, tb), 0)
    jj = lax.broadcasted_iota(jnp.int32, (tb, tb), 1)
    causal = ii >= jj
    strict = ii > jj
    gain = na_ref[...]

    def head_cols(x, h):
        return x[:, h * d:(h + 1) * d]

    q = [head_cols(q_all, h) for h in heads]
    k = [head_cols(k_all, h) for h in heads]
    beta =[jnp.sum(jnp.where(lane == h, gt, 0.0), axis=1, keepdims=True) for h in heads]
    gc = [jnp.sum(jnp.where(lane == h + n_heads, gt, 0.0), axis=1, keepdims=True) for h in heads]
    grow = gct_ref[0]
    decay = [jnp.where(causal, jnp.exp(jnp.where(causal, gc[h] - grow[h:h + 1], 0.0)), 0.0) for h in heads]
    kb = [k[h] * beta[h] for h in heads]
    eg = [jnp.exp(gc[h]) for h in heads]

    kq = [_dot_nt(jnp.concatenate([kb[h], q[h]], axis=0), k[h]) for h in heads]
    a = [jnp.where(strict, kq[h][:tb] * decay[h], 0.0) for h in heads]
    scores = [kq[h][tb:] * decay[h] for h in heads]
    t = _unit_lower_inverses(a, tb)
    uw = [_dot(t[h], jnp.concatenate([head_cols(v_all, h) * beta[h], kb[h] * eg[h]], axis=1)) for h in heads]
    s = [s_ref[h] for h in heads]
    ws = [_dot(jnp.concatenate([uw[h][:, d:], q[h] * eg[h]], axis=0), s[h]) for h in heads]
    v_new = [uw[h][:, :d] - ws[h][:tb] for h in heads]
    o = [ws[h][tb:] + _dot(scores[h], v_new[h]) for h in heads]
    g_last = [gc[h][tb - 1:tb] for h in heads]
    kv = [_dot_tn(k[h] * jnp.exp(g_last[h] - gc[h]), v_new[h]) for h in heads]
    for h in heads:
        s_ref[h] = s[h] * jnp.exp(g_last[h]) + kv[h]
        on = o[h] * lax.rsqrt(jnp.mean(o[h] * o[h], axis=-1, keepdims=True) + NORM_EPS) * gain
        o_ref[0, :, h * d:(h + 1) * d] = (on * head_cols(z_all, h)).astype(o_ref.dtype)


def _mixer_a(q, k, v, z, gates, gates_t, norm_a, n_heads):
    b, t, hd = q.shape
    d = A_HEAD_DIM
    tb = min(A_CHUNK, t)
    hh = n_heads
    col = pl.BlockSpec((1, tb, hd), lambda bi, ti: (bi, ti, 0))
    return pl.pallas_call(
        functools.partial(_mixer_a_kernel, tb=tb, n_heads=hh),
        out_shape=jax.ShapeDtypeStruct((b, t, hd), BF16),
        grid=(b, t // tb),
        in_specs=[col, col, col, col,
                  pl.BlockSpec((1, tb, gates.shape[-1]), lambda bi, ti: (bi, ti, 0)),
                  pl.BlockSpec((1, hh, tb), lambda bi, ti: (bi, 0, ti)),
                  pl.BlockSpec((1, d), lambda bi, ti: (0, 0))],
        out_specs=col,
        scratch_shapes=[pltpu.VMEM((hh, d, d), F32)],
        compiler_params=_params(("parallel", "arbitrary")),
        name="mixer_deltanet",
    )(q, k, v, z, gates, gates_t, norm_a.reshape(1, d).astype(F32))


def _mixer_b_kernel(q_ref, k_ref, v_ref, g_ref, nb_ref, o_ref, s_ref, decay_ref, *, chunk, n_heads):
    d = B_HEAD_DIM
    heads = range(n_heads)
    log_gamma = [math.log1p(-(2.0 ** (-5.0 - h))) for h in heads]

    @pl.when(pl.program_id(1) == 0)
    def _init():
        s_ref[...] = jnp.zeros_like(s_ref)
        ii = lax.broadcasted_iota(jnp.int32, (chunk, chunk), 0)
        jj = lax.broadcasted_iota(jnp.int32, (chunk, chunk), 1)
        causal = ii >= jj
        dist = jnp.where(causal, ii - jj, 0).astype(F32)
        for h in heads:
            decay_ref[h] = jnp.where(causal, jnp.exp(dist * log_gamma[h]), 0.0)

    pos = lax.broadcasted_iota(jnp.int32, (chunk, 1), 0).astype(F32)
    q_all, k_all, v_all, g_all = q_ref[0], k_ref[0], v_ref[0], g_ref[0]

    def head_cols(x, h):
        return x[:, h * d:(h + 1) * d]

    q = [head_cols(q_all, h) for h in heads]
    k = [head_cols(k_all, h) for h in heads]
    v = [head_cols(v_all, h) for h in heads]
    s = [s_ref[h] for h in heads]
    scores = [_dot_nt(q[h], k[h]) * decay_ref[h] for h in heads]
    cross = [_dot(q[h] * jnp.exp((pos + 1.0) * log_gamma[h]), s[h]) for h in heads]
    o = [_dot(scores[h], v[h]) + cross[h] for h in heads]
    kv = [_dot_tn(k[h] * jnp.exp((chunk - 1.0 - pos) * log_gamma[h]), v[h]) for h in heads]
    gain = nb_ref[...]
    for h in heads:
        s_ref[h] = s[h] * math.exp(chunk * log_gamma[h]) + kv[h]
        on = o[h] * lax.rsqrt(jnp.mean(o[h] * o[h], axis=-1, keepdims=True) + NORM_EPS) * gain
        o_ref[0, :, h * d:(h + 1) * d] = (on * head_cols(g_all, h)).astype(o_ref.dtype)


def _mixer_b(q, k, v, g, norm_b, n_heads):
    b, t, hd = q.shape
    d = B_HEAD_DIM
    chunk = min(B_CHUNK, t)
    hh = n_heads
    col = pl.BlockSpec((1, chunk, hd), lambda bi, ti: (bi, ti, 0))
    return pl.pallas_call(
        functools.partial(_mixer_b_kernel, chunk=chunk, n_heads=hh),
        out_shape=jax.ShapeDtypeStruct((b, t, hd), BF16),
        grid=(b, t // chunk),
        in_specs=[col, col, col, col, pl.BlockSpec((1, d), lambda bi, ti: (0, 0))],
        out_specs=col,
        scratch_shapes=[pltpu.VMEM((hh, d, d), F32), pltpu.VMEM((hh, chunk, chunk), F32)],
        compiler_params=_params(("arbitrary", "arbitrary")),
        name="mixer_retention",
    )(q, k, v, g, norm_b.reshape(1, d).astype(F32))


def _mixer_c_kernel(q_ref, k_ref, v_ref, r_ref, lr_ref, wup_ref, bgk_ref, nc_ref, o_ref, st_ref, *, chunk, n_heads):
    heads = range(n_heads)

    @pl.when(pl.program_id(1) == 0)
    def _init():
        st_ref[...] = jnp.zeros_like(st_ref)

    dk = q_ref.shape[-1] // n_heads
    dv = v_ref.shape[-1] // n_heads
    def kcols(h):
        return slice(h * dk, (h + 1) * dk)

    def vcols(h):
        return slice(h * dv, (h + 1) * dv)

    lr = lr_ref[0]
    cum_mask = _chunk_cumsum_mask(chunk, chunk)
    logit = [_dot(lr, wup_ref[:, kcols(h)]) + bgk_ref[:, kcols(h)] for h in heads]
    gk = [-_softplus(-logit[h]) / GK_NORMALIZER for h in heads]
    gc = [_split_dot(cum_mask, gk[h], 2) for h in heads]
    sub = min(C_SUBCHUNK, chunk)
    ii = lax.broadcasted_iota(jnp.int32, (sub, sub), 0)
    jj = lax.broadcasted_iota(jnp.int32, (sub, sub), 1)
    causal = ii >= jj

    q = [q_ref[0, :, kcols(h)] for h in heads]
    k = [k_ref[0, :, kcols(h)] for h in heads]
    v = [v_ref[0, :, vcols(h)] for h in heads]
    g_last = [gc[h][chunk - 1:chunk] for h in heads]
    st = [st_ref[h] for h in heads]

    def score_rows(h, r0):
        rows = slice(r0, r0 + sub)
        mid = gc[h][r0 + sub // 2 - 1:r0 + sub // 2]
        diag = jnp.where(causal, _dot_nt(q[h][rows] * jnp.exp(gc[h][rows] - mid),
                                         k[h][rows] * jnp.exp(mid - gc[h][rows])), 0.0)
        parts = [diag]
        if r0 > 0:
            bnd = gc[h][r0 - 1:r0]
            parts.insert(0, _dot_nt(q[h][rows] * jnp.exp(gc[h][rows] - bnd), k[h][:r0] * jnp.exp(bnd - gc[h][:r0])))
        if r0 + sub < chunk:
            parts.append(jnp.zeros((sub, chunk - r0 - sub), F32))
        return jnp.concatenate(parts, axis=1) if len(parts) > 1 else diag

    score_blocks = [[score_rows(h, r0) for h in heads] for r0 in range(0, chunk, sub)]
    scores = [jnp.concatenate([blk[h] for blk in score_blocks], axis=0) if len(score_blocks) > 1
              else score_blocks[0][h] for h in heads]
    cross = [_dot_nt(q[h] * jnp.exp(gc[h]), st[h]) for h in heads]
    o = [_dot(scores[h], v[h]) + cross[h] for h in heads]
    kv = [_dot_tn(v[h], k[h] * jnp.exp(g_last[h] - gc[h])) for h in heads]
    gain = nc_ref[...]
    for h in heads:
        st_ref[h] = st[h] * jnp.exp(g_last[h]) + kv[h]
        on = o[h] * lax.rsqrt(jnp.mean(o[h] * o[h], axis=-1, keepdims=True) + NORM_EPS) * gain
        o_ref[0, :, vcols(h)] = (on * r_ref[0, :, vcols(h)]).astype(o_ref.dtype)


def _mixer_c(q, k, v, r, lr, w_up, b_gk, norm_c):
    b, t, kw = q.shape
    vw = v.shape[-1]
    hh = C_HEADS
    dk, dv = kw // hh, vw // hh
    chunk = min(C_CHUNK, t)
    rw = lr.shape[-1]
    kcol = pl.BlockSpec((1, chunk, kw), lambda bi, ti: (bi, ti, 0))
    vcol = pl.BlockSpec((1, chunk, vw), lambda bi, ti: (bi, ti, 0))
    return pl.pallas_call(
        functools.partial(_mixer_c_kernel, chunk=chunk, n_heads=hh),
        out_shape=jax.ShapeDtypeStruct((b, t, vw), BF16),
        grid=(b, t // chunk),
        in_specs=[kcol, kcol, vcol, vcol,
                  pl.BlockSpec((1, chunk, rw), lambda bi, ti: (bi, ti, 0)),
                  pl.BlockSpec((rw, kw), lambda bi, ti: (0, 0)),
                  pl.BlockSpec((1, kw), lambda bi, ti: (0, 0)),
                  pl.BlockSpec((1, dv), lambda bi, ti: (0, 0))],
        out_specs=vcol,
        scratch_shapes=[pltpu.VMEM((hh, dv, dk), F32)],
        compiler_params=_params(("parallel", "arbitrary")),
        name="mixer_gla",
    )(q, k, v, r, lr, w_up, b_gk.reshape(1, kw).astype(F32), norm_c.reshape(1, dv).astype(F32))


def _pad_cols(w, width):
    return jnp.pad(w, ((0, 0), (0, width - w.shape[1])))


def _mlp(x2d, xb, ssq, w_up_g, w_down, layer, emit_stats):
    hid = _matmul(xb, w_up_g, layer=layer, row_ssq=ssq, act="relu2", out_dtype=BF16)
    return _matmul(hid, w_down, layer=layer, res=x2d, tk=4096, emit_stats=emit_stats)


def _even_layer(x2d, b, t, gain, w_in, conv_w, a_log, dt_bias, norm_a, norm_b, w_out):
    d = x2d.shape[1]
    ha, hb = d // 256, d // 512
    akw = ha * A_HEAD_DIM
    bkw = hb * B_HEAD_DIM
    small0 = 4 * akw
    b0 = small0 + 2 * ha
    conv_w = conv_w.astype(F32)

    def seq(a2d):
        return a2d.reshape(b, t, -1)

    hn = _rmsnorm(x2d, gain, BF16)
    w_bf = w_in.astype(BF16)

    def w_slice(lo, width):
        return w_bf[:, lo:lo + width]

    q_a = seq(_conv_proj(hn, w_bf, conv_w, (0, akw), t, norm_dim=A_HEAD_DIM, scale=A_HEAD_DIM ** -0.5))
    k_a = seq(_conv_proj(hn, w_bf, conv_w, (akw, akw), t, norm_dim=A_HEAD_DIM))
    v_a = seq(_conv_proj(hn, w_bf, conv_w, (2 * akw, akw), t))
    z_a = seq(_matmul(hn, w_bf, cols=(3 * akw, akw), act="silu"))
    ba = _matmul(hn, _pad_cols(w_slice(small0, 2 * ha), LANES))
    gates = _gates(ba, a_log, dt_bias, ha, A_CHUNK).reshape(b, t, LANES)
    gates_t = jnp.swapaxes(gates[:, :, ha:2 * ha], 1, 2)
    o_a = _mixer_a(q_a, k_a, v_a, z_a, gates, gates_t, norm_a, ha)
    cos, sin = _rope_tables(t, B_HEAD_DIM)
    q_b = seq(_rope_proj(hn, w_slice(b0, bkw), cos, sin, t, B_HEAD_DIM))
    k_b = seq(_rope_proj(hn, w_slice(b0 + bkw, bkw), cos, sin, t, B_HEAD_DIM, scale=B_HEAD_DIM ** -0.5))
    v_b = seq(_matmul(hn, w_slice(b0 + 2 * bkw, bkw)))
    g_b = seq(_matmul(hn, w_slice(b0 + 3 * bkw, bkw), act="silu"))
    o_b = _mixer_b(q_b, k_b, v_b, g_b, norm_b, hb)
    return _matmul(o_a.reshape(b * t, akw), w_out.astype(BF16), a2=o_b.reshape(b * t, bkw), res=x2d,
                   emit_stats=True)


def _odd_layer(x2d, xb, ssq, b, t, gain, w_in, w_gk_down, w_gk_up, b_gk, norm_c, w_out):
    d = x2d.shape[1]
    kw, vw = d // 2, d
    dk = kw // C_HEADS

    def seq(a2d):
        return a2d.reshape(b, t, -1)

    gcol = gain.astype(F32)[:, None]
    w_bf = (w_in * gcol).astype(BF16)
    q = seq(_matmul(xb, w_bf, cols=(0, kw), row_ssq=ssq, act=dk ** -0.5))
    k = seq(_matmul(xb, w_bf, cols=(kw, kw), row_ssq=ssq))
    v = seq(_matmul(xb, w_bf, cols=(2 * kw, vw), row_ssq=ssq))
    r = seq(_matmul(xb, w_bf, cols=(2 * kw + vw, vw), row_ssq=ssq, act="silu"))
    lr = seq(_matmul(xb, _pad_cols(w_gk_down * gcol, LANES).astype(BF16), row_ssq=ssq))
    w_up = jnp.pad(w_gk_up, ((0, LANES - w_gk_up.shape[0]), (0, 0))).astype(BF16)
    o_c = _mixer_c(q, k, v, r, lr, w_up, b_gk, norm_c).reshape(b * t, vw)
    return _matmul(o_c, w_out.astype(BF16), res=x2d, emit_stats=True)


def kernel(x, norm_mix, norm_mlp, norm_final, w_up, w_down, w_in_ab, conv_a, a_log, dt_bias, norm_a, norm_b,
           w_out_ab, w_in_c, w_gk_down, w_gk_up, b_gk, norm_c, w_out_c):
    b, t, d = x.shape
    depth = norm_mix.shape[0]
    x2d = x.reshape(b * t, d)
    w_up_g = (w_up * norm_mlp.astype(F32)[:, :, None]).astype(BF16)
    w_down = w_down.astype(BF16)
    xb = ssq = None
    for layer in range(depth):
        i = layer // 2
        if layer % 2 == 0:
            x2d, xb, ssq = _even_layer(x2d, b, t, norm_mix[layer], w_in_ab[i], conv_a[i], a_log[i], dt_bias[i],
                                       norm_a[i], norm_b[i], w_out_ab[i])
        else:
            x2d, xb, ssq = _odd_layer(x2d, xb, ssq, b, t, norm_mix[layer], w_in_c[i], w_gk_down[i], w_gk_up[i],
                                      b_gk[i], norm_c[i], w_out_c[i])
        if layer + 1 < depth:
            x2d, xb, ssq = _mlp(x2d, xb, ssq, w_up_g, w_down, layer, emit_stats=True)
        else:
            x2d = _mlp(x2d, xb, ssq, w_up_g, w_down, layer, emit_stats=False)
    return _rmsnorm(x2d, norm_final, F32).reshape(b, t, d)
```

```python
import functools
import math

import jax
import jax.numpy as jnp
from jax import lax
from jax.experimental import pallas as pl
from jax.experimental.pallas import tpu as pltpu

F32 = jnp.float32
BF16 = jnp.bfloat16

NORM_EPS = 1e-6
ROPE_BASE = 10000.0
CONV_K = 4
GK_NORMALIZER = 16.0
A_HEAD_DIM = 128
B_HEAD_DIM = 256
C_HEADS = 4
LANES = 128
SUBLANES = 8
MXU_COLS = 256
CONV_PROJ_ROWS = 512
A_CHUNK = 128
B_CHUNK = 256
C_CHUNK = 256
C_SUBCHUNK = 128
MIB = 1024 * 1024
VMEM_MIB_WITH_STATS = 62


def _params(semantics, vmem_mib=None):
    kwargs = dict(dimension_semantics=semantics)
    if vmem_mib is not None:
        kwargs["vmem_limit_bytes"] = vmem_mib * MIB
    return pltpu.CompilerParams(**kwargs)


def _dot(a, b):
    return jnp.dot(a.astype(BF16), b.astype(BF16), preferred_element_type=F32)


def _dot_nt(a, b):
    return lax.dot_general(a.astype(BF16), b.astype(BF16), (((1,), (1,)), ((), ())),
                           preferred_element_type=F32)


def _dot_tn(a, b):
    return lax.dot_general(a.astype(BF16), b.astype(BF16), (((0,), (0,)), ((), ())),
                           preferred_element_type=F32)


def _sigmoid(x):
    return 1.0 / (1.0 + jnp.exp(-x))


def _silu(x):
    return x * _sigmoid(x)


def _softplus(x):
    return jnp.maximum(x, 0.0) + jnp.log1p(jnp.exp(-jnp.abs(x)))


def _split_dot(mask_bf16, g, pieces):
    acc = None
    rem = g
    for _ in range(pieces):
        part = rem.astype(BF16)
        term = jnp.dot(mask_bf16, part, preferred_element_type=F32)
        acc = term if acc is None else acc + term
        rem = rem - part.astype(F32)
    return acc


def _chunk_cumsum_mask(n, chunk):
    ii = lax.broadcasted_iota(jnp.int32, (n, n), 0)
    jj = lax.broadcasted_iota(jnp.int32, (n, n), 1)
    same = (ii // chunk) == (jj // chunk)
    return jnp.where(same & (ii >= jj), 1.0, 0.0).astype(BF16)


def _rmsnorm_kernel(x_ref, g_ref, o_ref):
    x = x_ref[...]
    ms = jnp.mean(x * x, axis=-1, keepdims=True)
    o_ref[...] = (x * lax.rsqrt(ms + NORM_EPS) * g_ref[...]).astype(o_ref.dtype)


def _rmsnorm(x2d, gain, out_dtype, tm=512):
    m, d = x2d.shape
    tm = min(tm, m)
    return pl.pallas_call(
        _rmsnorm_kernel,
        out_shape=jax.ShapeDtypeStruct((m, d), out_dtype),
        grid=(m // tm,),
        in_specs=[pl.BlockSpec((tm, d), lambda i: (i, 0)),
                  pl.BlockSpec((1, d), lambda i: (0, 0))],
        out_specs=pl.BlockSpec((tm, d), lambda i: (i, 0)),
        compiler_params=_params(("parallel",)),
        name="rmsnorm",
    )(x2d, gain.reshape(1, d).astype(F32))


def _matmul_kernel(*refs, nk, n_pairs, act, has_res, has_scale, emit_stats, norm_dim):
    ab_refs = refs[:2 * n_pairs]
    pos = 2 * n_pairs
    ssq_in_ref = refs[pos] if has_scale else None
    pos += int(has_scale)
    res_ref = refs[pos] if has_res else None
    pos += int(has_res)
    o_ref = refs[pos]
    xb_ref, ssq_out_ref = (refs[pos + 1], refs[pos + 2]) if emit_stats else (None, None)
    j = pl.program_id(1)

    def product():
        acc = None
        for p in range(n_pairs):
            term = jnp.dot(ab_refs[2 * p][...], ab_refs[2 * p + 1][...], preferred_element_type=F32)
            acc = term if acc is None else acc + term
        return acc

    def stats(x_new):
        xb_ref[...] = x_new.astype(xb_ref.dtype)
        part = jnp.broadcast_to(jnp.sum(x_new * x_new, axis=-1, keepdims=True), ssq_out_ref.shape)
        ssq_out_ref[...] = jnp.where(j == 0, part, ssq_out_ref[...] + part)

    if nk == 1:
        acc = product()
        if has_scale:
            acc = acc * lax.rsqrt(ssq_in_ref[:, 0:1] * (1.0 / norm_dim) + NORM_EPS)
        if act == "relu2":
            r = jnp.maximum(acc, 0.0)
            acc = r * r
        elif act == "silu":
            acc = _silu(acc)
        elif isinstance(act, float):
            acc = acc * act
        if has_res:
            acc = res_ref[...] + acc
        o_ref[...] = acc.astype(o_ref.dtype)
        if emit_stats:
            stats(acc)
    else:
        k = pl.program_id(2)

        @pl.when(k == 0)
        def _first():
            o_ref[...] = res_ref[...] if has_res else jnp.zeros_like(o_ref)

        o_ref[...] += product()
        if emit_stats:
            @pl.when(k == nk - 1)
            def _last():
                stats(o_ref[...])


def _matmul(a, b, *, layer=None, cols=None, a2=None, row_ssq=None, res=None, act=None, emit_stats=False,
            out_dtype=F32, tm=1024, tn=1024, tk=None, vmem_mib=56):
    m, ka = a.shape
    col0, n = (0, b.shape[-1]) if cols is None else cols
    n_pairs = 1 if a2 is None else 2
    tm, tn = min(tm, m), min(tn, n)
    tk = ka if tk is None else min(tk, ka)
    assert m % tm == 0 and n % tn == 0 and ka % tk == 0 and col0 % tn == 0
    jb0 = col0 // tn
    nk = ka // tk
    has_res = res is not None
    has_scale = row_ssq is not None
    assert nk == 1 or (act is None and out_dtype == F32 and a2 is None and not has_scale)
    stacked = layer is not None

    def a_map(i, j, *k):
        return (i, k[0] if k else 0)

    def o_map(i, j, *k):
        return (i, j)

    def row_map(i, j, *k):
        return (i, 0)

    def b_spec(row_block):
        def b_map(i, j, *k):
            kb = (k[0] if k else 0) + row_block
            return (layer, kb, j + jb0) if stacked else (kb, j + jb0)
        return pl.BlockSpec((None, tk, tn) if stacked else (tk, tn), b_map)

    a_spec = pl.BlockSpec((tm, tk), a_map)
    o_spec = pl.BlockSpec((tm, tn), o_map)
    ssq_spec = pl.BlockSpec((tm, LANES), row_map)
    in_specs = [a_spec, b_spec(0)]
    args = [a, b]
    if a2 is not None:
        in_specs += [a_spec, b_spec(1)]
        args += [a2, b]
    if has_scale:
        in_specs.append(ssq_spec)
        args.append(row_ssq)
    if has_res:
        in_specs.append(o_spec)
        args.append(res)
    out_shape = jax.ShapeDtypeStruct((m, n), out_dtype)
    out_specs = o_spec
    if emit_stats:
        out_shape = (out_shape, jax.ShapeDtypeStruct((m, n), BF16), jax.ShapeDtypeStruct((m, LANES), F32))
        out_specs = (o_spec, o_spec, ssq_spec)
    if emit_stats:
        vmem_mib = max(vmem_mib, VMEM_MIB_WITH_STATS)
    grid = (m // tm, n // tn) + ((nk,) if nk > 1 else ())
    sem = ("parallel", "arbitrary" if emit_stats else "parallel") + (("arbitrary",) if nk > 1 else ())
    return pl.pallas_call(
        functools.partial(_matmul_kernel, nk=nk, n_pairs=n_pairs, act=act, has_res=has_res, has_scale=has_scale,
                          emit_stats=emit_stats, norm_dim=ka),
        out_shape=out_shape,
        grid=grid,
        in_specs=in_specs,
        out_specs=out_specs,
        compiler_params=_params(sem, vmem_mib),
        name="matmul",
    )(*args)


def _conv_proj_kernel(a_ref, b_ref, cw_ref, o_ref, carry_ref, raw_ref, *, tiles_per_seq, norm_dim, scale):
    i, j = pl.program_id(0), pl.program_id(1)
    tm, tn = o_ref.shape
    rows_per_dot = min(CONV_PROJ_ROWS, tm)
    first_tile = lax.rem(i, tiles_per_seq) == 0

    prev = carry_ref[j]
    raw_ref[0:SUBLANES, :] = jnp.where(first_tile, jnp.zeros_like(prev), prev)

    def epilogue(r0, c0):
        cols = slice(c0, c0 + MXU_COLS)
        if r0 + rows_per_dot == tm:
            carry_ref[j, :, cols] = raw_ref[tm:tm + SUBLANES, cols]
        w = cw_ref[:, cols]
        y = w[CONV_K - 1:CONV_K] * raw_ref[SUBLANES + r0:SUBLANES + r0 + rows_per_dot, cols]
        for tap in range(CONV_K - 1):
            off = SUBLANES - (CONV_K - 1) + tap + r0
            y = y + w[tap:tap + 1] * raw_ref[off:off + rows_per_dot, cols]
        y = _silu(y)
        if norm_dim is not None:
            segs = []
            for s0 in range(0, MXU_COLS, norm_dim):
                seg = y[:, s0:s0 + norm_dim]
                segs.append(seg * (lax.rsqrt(jnp.sum(seg * seg, axis=-1, keepdims=True) + NORM_EPS) * scale))
            y = jnp.concatenate(segs, axis=1)
        o_ref[r0:r0 + rows_per_dot, cols] = y.astype(o_ref.dtype)

    subtiles = [(r0, c0) for r0 in range(0, tm, rows_per_dot) for c0 in range(0, tn, MXU_COLS)]
    pending = None
    for r0, c0 in subtiles:
        raw_ref[SUBLANES + r0:SUBLANES + r0 + rows_per_dot, c0:c0 + MXU_COLS] = jnp.dot(
            a_ref[r0:r0 + rows_per_dot, :], b_ref[:, c0:c0 + MXU_COLS], preferred_element_type=F32)
        if pending is not None:
            epilogue(*pending)
        pending = (r0, c0)
    epilogue(*pending)


def _conv_proj(a, b, conv_w, cols, seq_len, *, norm_dim=None, scale=1.0, tm=1024, tn=1024, vmem_mib=56):
    m, kdim = a.shape
    col0, n = cols
    tm, tn = min(tm, seq_len), min(tn, n)
    assert m % tm == 0 and n % tn == 0 and seq_len % tm == 0 and col0 % tn == 0
    jb0 = col0 // tn
    return pl.pallas_call(
        functools.partial(_conv_proj_kernel, tiles_per_seq=seq_len // tm, norm_dim=norm_dim, scale=scale),
        out_shape=jax.ShapeDtypeStruct((m, n), F32),
        grid=(m // tm, n // tn),
        in_specs=[pl.BlockSpec((tm, kdim), lambda i, j: (i, 0)),
                  pl.BlockSpec((kdim, tn), lambda i, j: (0, j + jb0)),
                  pl.BlockSpec((CONV_K, tn), lambda i, j: (0, j + jb0))],
        out_specs=pl.BlockSpec((tm, tn), lambda i, j: (i, j)),
        scratch_shapes=[pltpu.VMEM((n // tn, SUBLANES, tn), F32), pltpu.VMEM((tm + SUBLANES, tn), F32)],
        compiler_params=_params(("arbitrary", "arbitrary"), vmem_mib),
        name="conv_proj",
    )(a, b, conv_w)


def _rope_proj_kernel(a_ref, b_ref, cos_ref, sin_ref, o_ref, *, head_dim, scale):
    acc = jnp.dot(a_ref[...], b_ref[...], preferred_element_type=F32)
    tm, tn = acc.shape
    even = (lax.broadcasted_iota(jnp.int32, (tm, LANES), 1) % 2) == 0
    parts = []
    for c0 in range(0, tn, LANES):
        x = acc[:, c0:c0 + LANES]
        partner = jnp.where(even, pltpu.roll(x, LANES - 1, axis=1), pltpu.roll(x, 1, axis=1))
        t0 = c0 % head_dim
        parts.append((x * cos_ref[:, t0:t0 + LANES] + partner * sin_ref[:, t0:t0 + LANES]) * scale)
    o_ref[...] = jnp.concatenate(parts, axis=1).astype(o_ref.dtype)


def _rope_proj(a, b, cos, sin, seq_len, head_dim, *, scale=1.0, tm=1024, tn=1024, vmem_mib=56):
    m, kdim = a.shape
    n = b.shape[1]
    tm, tn = min(tm, seq_len), min(tn, n)
    assert m % tm == 0 and n % tn == 0 and seq_len % tm == 0 and tn % head_dim == 0
    tiles_per_seq = seq_len // tm
    tab = pl.BlockSpec((tm, head_dim), lambda i, j: (i % tiles_per_seq, 0))
    return pl.pallas_call(
        functools.partial(_rope_proj_kernel, head_dim=head_dim, scale=scale),
        out_shape=jax.ShapeDtypeStruct((m, n), F32),
        grid=(m // tm, n // tn),
        in_specs=[pl.BlockSpec((tm, kdim), lambda i, j: (i, 0)),
                  pl.BlockSpec((kdim, tn), lambda i, j: (0, j)),
                  tab, tab],
        out_specs=pl.BlockSpec((tm, tn), lambda i, j: (i, j)),
        compiler_params=_params(("parallel", "parallel"), vmem_mib),
        name="rope_proj",
    )(a, b, cos, sin)


def _rope_kernel(inv_ref, sign_ref, cos_ref, sin_ref, *, tt):
    pos = (lax.broadcasted_iota(jnp.int32, cos_ref.shape, 0) + pl.program_id(0) * tt).astype(F32)
    ang = pos * inv_ref[...]
    cos_ref[...] = jnp.cos(ang)
    sin_ref[...] = jnp.sin(ang) * sign_ref[...]


def _rope_tables(t, head_dim, tt=512):
    tt = min(tt, t)
    half = head_dim // 2
    inv_freq = jnp.power(ROPE_BASE, -jnp.linspace(0.0, 1.0, half, dtype=F32))
    inv_pair = jnp.repeat(inv_freq, 2).reshape(1, head_dim)
    sign = jnp.tile(jnp.array([-1.0, 1.0], F32), half).reshape(1, head_dim)
    row = pl.BlockSpec((1, head_dim), lambda i: (0, 0))
    tab = pl.BlockSpec((tt, head_dim), lambda i: (i, 0))
    return pl.pallas_call(
        functools.partial(_rope_kernel, tt=tt),
        out_shape=(jax.ShapeDtypeStruct((t, head_dim), F32), jax.ShapeDtypeStruct((t, head_dim), F32)),
        grid=(t // tt,),
        in_specs=[row, row],
        out_specs=(tab, tab),
        compiler_params=_params(("parallel",)),
        name="rope_tables",
    )(inv_pair, sign)


def _gates_kernel(x_ref, alog_ref, dtb_ref, o_ref, *, n_heads, chunk):
    x = x_ref[...]
    beta = _sigmoid(x)
    g = -jnp.exp(alog_ref[...]) * _softplus(x + dtb_ref[...])
    gc = _split_dot(_chunk_cumsum_mask(x.shape[0], chunk), g, 3)
    lane = lax.broadcasted_iota(jnp.int32, x.shape, 1)
    o_ref[...] = jnp.where(lane < n_heads, beta, gc)


def _gates(ba, a_log, dt_bias, n_heads, chunk, tg=256):
    m, w = ba.shape
    tg = min(tg, m)
    alog_p = jnp.zeros((1, w), F32).at[0, n_heads:2 * n_heads].set(a_log.astype(F32))
    dtb_p = jnp.zeros((1, w), F32).at[0, n_heads:2 * n_heads].set(dt_bias.astype(F32))
    return pl.pallas_call(
        functools.partial(_gates_kernel, n_heads=n_heads, chunk=chunk),
        out_shape=jax.ShapeDtypeStruct((m, w), F32),
        grid=(m // tg,),
        in_specs=[pl.BlockSpec((tg, w), lambda i: (i, 0)),
                  pl.BlockSpec((1, w), lambda i: (0, 0)),
                  pl.BlockSpec((1, w), lambda i: (0, 0))],
        out_specs=pl.BlockSpec((tg, w), lambda i: (i, 0)),
        compiler_params=_params(("parallel",)),
        name="deltanet_gates",
    )(ba, alog_p, dtb_p)


def _unit_lower_inverses(mats, n):
    ii = lax.broadcasted_iota(jnp.int32, (n, n), 0)
    jj = lax.broadcasted_iota(jnp.int32, (n, n), 1)
    base = 16
    eye = jnp.where(ii == jj, 1.0, 0.0)
    diag_blocks = (ii // base) == (jj // base)
    ps = [jnp.where(diag_blocks, -a, 0.0) for a in mats]
    ts = [eye + p for p in ps]
    ps = [_dot(p, p) for p in ps]
    width = 4
    while width <= base:
        if width == base:
            ts = [t + _dot(t, p) for t, p in zip(ts, ps)]
        else:
            both = [_dot(jnp.concatenate([p, t], axis=0), p) for t, p in zip(ts, ps)]
            ts = [t + pt[n:] for t, pt in zip(ts, both)]
            ps = [pt[:n] for pt in both]
        width *= 2
    bs = base
    while bs < n:
        off_blocks = ((ii // (2 * bs)) == (jj // (2 * bs))) & ((ii // bs) != (jj // bs))
        xs = [_dot(t, jnp.where(off_blocks, a, 0.0)) for t, a in zip(ts, mats)]
        ts = [t - _dot(x, t) for t, x in zip(ts, xs)]
        bs *= 2
    return ts


def _mixer_a_kernel(q_ref, k_ref, v_ref, z_ref, gate_ref, gct_ref, na_ref, o_ref, s_ref, *, tb, n_heads):
    d = A_HEAD_DIM
    heads = range(n_heads)

    @pl.when(pl.program_id(1) == 0)
    def _init():
        s_ref[...] = jnp.zeros_like(s_ref)

    q_all, k_all, v_all, z_all = q_ref[0], k_ref[0], v_ref[0], z_ref[0]
    gt = gate_ref[0]
    lane = lax.broadcasted_iota(jnp.int32, gt.shape, 1)

    ii = lax.broadcasted_iota(jnp.int32, (tb, tb), 0)
    jj = lax.broadcasted_iota(jnp.int32, (tb, tb), 1)
    causal = ii >= jj
    strict = ii > jj
    gain = na_ref[...]

    def head_cols(x, h):
        return x[:, h * d:(h + 1) * d]

    q = [head_cols(q_all, h) for h in heads]
    k = [head_cols(k_all, h) for h in heads]
    beta =[jnp.sum(jnp.where(lane == h, gt, 0.0), axis=1, keepdims=True) for h in heads]
    gc = [jnp.sum(jnp.where(lane == h + n_heads, gt, 0.0), axis=1, keepdims=True) for h in heads]
    grow = gct_ref[0]
    decay = [jnp.where(causal, jnp.exp(jnp.where(causal, gc[h] - grow[h:h + 1], 0.0)), 0.0) for h in heads]
    kb = [k[h] * beta[h] for h in heads]
    eg = [jnp.exp(gc[h]) for h in heads]

    kq = [_dot_nt(jnp.concatenate([kb[h], q[h]], axis=0), k[h]) for h in heads]
    a = [jnp.where(strict, kq[h][:tb] * decay[h], 0.0) for h in heads]
    scores = [kq[h][tb:] * decay[h] for h in heads]
    t = _unit_lower_inverses(a, tb)
    uw = [_dot(t[h], jnp.concatenate([head_cols(v_all, h) * beta[h], kb[h] * eg[h]], axis=1)) for h in heads]
    s = [s_ref[h] for h in heads]
    ws = [_dot(jnp.concatenate([uw[h][:, d:], q[h] * eg[h]], axis=0), s[h]) for h in heads]
    v_new = [uw[h][:, :d] - ws[h][:tb] for h in heads]
    o = [ws[h][tb:] + _dot(scores[h], v_new[h]) for h in heads]
    g_last = [gc[h][tb - 1:tb] for h in heads]
    kv = [_dot_tn(k[h] * jnp.exp(g_last[h] - gc[h]), v_new[h]) for h in heads]
    for h in heads:
        s_ref[h] = s[h] * jnp.exp(g_last[h]) + kv[h]
        on = o[h] * lax.rsqrt(jnp.mean(o[h] * o[h], axis=-1, keepdims=True) + NORM_EPS) * gain
        o_ref[0, :, h * d:(h + 1) * d] = (on * head_cols(z_all, h)).astype(o_ref.dtype)


def _mixer_a(q, k, v, z, gates, gates_t, norm_a, n_heads):
    b, t, hd = q.shape
    d = A_HEAD_DIM
    tb = min(A_CHUNK, t)
    hh = n_heads
    col = pl.BlockSpec((1, tb, hd), lambda bi, ti: (bi, ti, 0))
    return pl.pallas_call(
        functools.partial(_mixer_a_kernel, tb=tb, n_heads=hh),
        out_shape=jax.ShapeDtypeStruct((b, t, hd), BF16),
        grid=(b, t // tb),
        in_specs=[col, col, col, col,
                  pl.BlockSpec((1, tb, gates.shape[-1]), lambda bi, ti: (bi, ti, 0)),
                  pl.BlockSpec((1, hh, tb), lambda bi, ti: (bi, 0, ti)),
                  pl.BlockSpec((1, d), lambda bi, ti: (0, 0))],
        out_specs=col,
        scratch_shapes=[pltpu.VMEM((hh, d, d), F32)],
        compiler_params=_params(("parallel", "arbitrary")),
        name="mixer_deltanet",
    )(q, k, v, z, gates, gates_t, norm_a.reshape(1, d).astype(F32))


def _mixer_b_kernel(q_ref, k_ref, v_ref, g_ref, nb_ref, o_ref, s_ref, decay_ref, *, chunk, n_heads):
    d = B_HEAD_DIM
    heads = range(n_heads)
    log_gamma = [math.log1p(-(2.0 ** (-5.0 - h))) for h in heads]

    @pl.when(pl.program_id(1) == 0)
    def _init():
        s_ref[...] = jnp.zeros_like(s_ref)
        ii = lax.broadcasted_iota(jnp.int32, (chunk, chunk), 0)
        jj = lax.broadcasted_iota(jnp.int32, (chunk, chunk), 1)
        causal = ii >= jj
        dist = jnp.where(causal, ii - jj, 0).astype(F32)
        for h in heads:
            decay_ref[h] = jnp.where(causal, jnp.exp(dist * log_gamma[h]), 0.0)

    pos = lax.broadcasted_iota(jnp.int32, (chunk, 1), 0).astype(F32)
    q_all, k_all, v_all, g_all = q_ref[0], k_ref[0], v_ref[0], g_ref[0]

    def head_cols(x, h):
        return x[:, h * d:(h + 1) * d]

    q = [head_cols(q_all, h) for h in heads]
    k = [head_cols(k_all, h) for h in heads]
    v = [head_cols(v_all, h) for h in heads]
    s = [s_ref[h] for h in heads]
    scores = [_dot_nt(q[h], k[h]) * decay_ref[h] for h in heads]
    cross = [_dot(q[h] * jnp.exp((pos + 1.0) * log_gamma[h]), s[h]) for h in heads]
    o = [_dot(scores[h], v[h]) + cross[h] for h in heads]
    kv = [_dot_tn(k[h] * jnp.exp((chunk - 1.0 - pos) * log_gamma[h]), v[h]) for h in heads]
    gain = nb_ref[...]
    for h in heads:
        s_ref[h] = s[h] * math.exp(chunk * log_gamma[h]) + kv[h]
        on = o[h] * lax.rsqrt(jnp.mean(o[h] * o[h], axis=-1, keepdims=True) + NORM_EPS) * gain
        o_ref[0, :, h * d:(h + 1) * d] = (on * head_cols(g_all, h)).astype(o_ref.dtype)


def _mixer_b(q, k, v, g, norm_b, n_heads):
    b, t, hd = q.shape
    d = B_HEAD_DIM
    chunk = min(B_CHUNK, t)
    hh = n_heads
    col = pl.BlockSpec((1, chunk, hd), lambda bi, ti: (bi, ti, 0))
    return pl.pallas_call(
        functools.partial(_mixer_b_kernel, chunk=chunk, n_heads=hh),
        out_shape=jax.ShapeDtypeStruct((b, t, hd), BF16),
        grid=(b, t // chunk),
        in_specs=[col, col, col, col, pl.BlockSpec((1, d), lambda bi, ti: (0, 0))],
        out_specs=col,
        scratch_shapes=[pltpu.VMEM((hh, d, d), F32), pltpu.VMEM((hh, chunk, chunk), F32)],
        compiler_params=_params(("arbitrary", "arbitrary")),
        name="mixer_retention",
    )(q, k, v, g, norm_b.reshape(1, d).astype(F32))


def _mixer_c_kernel(q_ref, k_ref, v_ref, r_ref, lr_ref, wup_ref, bgk_ref, nc_ref, o_ref, st_ref, *, chunk, n_heads):
    heads = range(n_heads)

    @pl.when(pl.program_id(1) == 0)
    def _init():
        st_ref[...] = jnp.zeros_like(st_ref)

    dk = q_ref.shape[-1] // n_heads
    dv = v_ref.shape[-1] // n_heads
    def kcols(h):
        return slice(h * dk, (h + 1) * dk)

    def vcols(h):
        return slice(h * dv, (h + 1) * dv)

    lr = lr_ref[0]
    cum_mask = _chunk_cumsum_mask(chunk, chunk)
    logit = [_dot(lr, wup_ref[:, kcols(h)]) + bgk_ref[:, kcols(h)] for h in heads]
    gk = [-_softplus(-logit[h]) / GK_NORMALIZER for h in heads]
    gc = [_split_dot(cum_mask, gk[h], 2) for h in heads]
    sub = min(C_SUBCHUNK, chunk)
    ii = lax.broadcasted_iota(jnp.int32, (sub, sub), 0)
    jj = lax.broadcasted_iota(jnp.int32, (sub, sub), 1)
    causal = ii >= jj

    q = [q_ref[0, :, kcols(h)] for h in heads]
    k = [k_ref[0, :, kcols(h)] for h in heads]
    v = [v_ref[0, :, vcols(h)] for h in heads]
    g_last = [gc[h][chunk - 1:chunk] for h in heads]
    st = [st_ref[h] for h in heads]

    def score_rows(h, r0):
        rows = slice(r0, r0 + sub)
        mid = gc[h][r0 + sub // 2 - 1:r0 + sub // 2]
        diag = jnp.where(causal, _dot_nt(q[h][rows] * jnp.exp(gc[h][rows] - mid),
                                         k[h][rows] * jnp.exp(mid - gc[h][rows])), 0.0)
        parts = [diag]
        if r0 > 0:
            bnd = gc[h][r0 - 1:r0]
            parts.insert(0, _dot_nt(q[h][rows] * jnp.exp(gc[h][rows] - bnd), k[h][:r0] * jnp.exp(bnd - gc[h][:r0])))
        if r0 + sub < chunk:
            parts.append(jnp.zeros((sub, chunk - r0 - sub), F32))
        return jnp.concatenate(parts, axis=1) if len(parts) > 1 else diag

    score_blocks = [[score_rows(h, r0) for h in heads] for r0 in range(0, chunk, sub)]
    scores = [jnp.concatenate([blk[h] for blk in score_blocks], axis=0) if len(score_blocks) > 1
              else score_blocks[0][h] for h in heads]
    cross = [_dot_nt(q[h] * jnp.exp(gc[h]), st[h]) for h in heads]
    o = [_dot(scores[h], v[h]) + cross[h] for h in heads]
    kv = [_dot_tn(v[h], k[h] * jnp.exp(g_last[h] - gc[h])) for h in heads]
    gain = nc_ref[...]
    for h in heads:
        st_ref[h] = st[h] * jnp.exp(g_last[h]) + kv[h]
        on = o[h] * lax.rsqrt(jnp.mean(o[h] * o[h], axis=-1, keepdims=True) + NORM_EPS) * gain
        o_ref[0, :, vcols(h)] = (on * r_ref[0, :, vcols(h)]).astype(o_ref.dtype)


def _mixer_c(q, k, v, r, lr, w_up, b_gk, norm_c):
    b, t, kw = q.shape
    vw = v.shape[-1]
    hh = C_HEADS
    dk, dv = kw // hh, vw // hh
    chunk = min(C_CHUNK, t)
    rw = lr.shape[-1]
    kcol = pl.BlockSpec((1, chunk, kw), lambda bi, ti: (bi, ti, 0))
    vcol = pl.BlockSpec((1, chunk, vw), lambda bi, ti: (bi, ti, 0))
    return pl.pallas_call(
        functools.partial(_mixer_c_kernel, chunk=chunk, n_heads=hh),
        out_shape=jax.ShapeDtypeStruct((b, t, vw), BF16),
        grid=(b, t // chunk),
        in_specs=[kcol, kcol, vcol, vcol,
                  pl.BlockSpec((1, chunk, rw), lambda bi, ti: (bi, ti, 0)),
                  pl.BlockSpec((rw, kw), lambda bi, ti: (0, 0)),
                  pl.BlockSpec((1, kw), lambda bi, ti: (0, 0)),
                  pl.BlockSpec((1, dv), lambda bi, ti: (0, 0))],
        out_specs=vcol,
        scratch_shapes=[pltpu.VMEM((hh, dv, dk), F32)],
        compiler_params=_params(("parallel", "arbitrary")),
        name="mixer_gla",
    )(q, k, v, r, lr, w_up, b_gk.reshape(1, kw).astype(F32), norm_c.reshape(1, dv).astype(F32))


def _pad_cols(w, width):
    return jnp.pad(w, ((0, 0), (0, width - w.shape[1])))


def _mlp(x2d, xb, ssq, w_up_g, w_down, layer, emit_stats):
    hid = _matmul(xb, w_up_g, layer=layer, row_ssq=ssq, act="relu2", out_dtype=BF16)
    return _matmul(hid, w_down, layer=layer, res=x2d, tk=4096, emit_stats=emit_stats)


def _even_layer(x2d, b, t, gain, w_in, conv_w, a_log, dt_bias, norm_a, norm_b, w_out):
    d = x2d.shape[1]
    ha, hb = d // 256, d // 512
    akw = ha * A_HEAD_DIM
    bkw = hb * B_HEAD_DIM
    small0 = 4 * akw
    b0 = small0 + 2 * ha
    conv_w = conv_w.astype(F32)

    def seq(a2d):
        return a2d.reshape(b, t, -1)

    hn = _rmsnorm(x2d, gain, BF16)
    w_bf = w_in.astype(BF16)

    def w_slice(lo, width):
        return w_bf[:, lo:lo + width]

    q_a = seq(_conv_proj(hn, w_bf, conv_w, (0, akw), t, norm_dim=A_HEAD_DIM, scale=A_HEAD_DIM ** -0.5))
    k_a = seq(_conv_proj(hn, w_bf, conv_w, (akw, akw), t, norm_dim=A_HEAD_DIM))
    v_a = seq(_conv_proj(hn, w_bf, conv_w, (2 * akw, akw), t))
    z_a = seq(_matmul(hn, w_bf, cols=(3 * akw, akw), act="silu"))
    ba = _matmul(hn, _pad_cols(w_slice(small0, 2 * ha), LANES))
    gates = _gates(ba, a_log, dt_bias, ha, A_CHUNK).reshape(b, t, LANES)
    gates_t = jnp.swapaxes(gates[:, :, ha:2 * ha], 1, 2)
    o_a = _mixer_a(q_a, k_a, v_a, z_a, gates, gates_t, norm_a, ha)
    cos, sin = _rope_tables(t, B_HEAD_DIM)
    q_b = seq(_rope_proj(hn, w_slice(b0, bkw), cos, sin, t, B_HEAD_DIM))
    k_b = seq(_rope_proj(hn, w_slice(b0 + bkw, bkw), cos, sin, t, B_HEAD_DIM, scale=B_HEAD_DIM ** -0.5))
    v_b = seq(_matmul(hn, w_slice(b0 + 2 * bkw, bkw)))
    g_b = seq(_matmul(hn, w_slice(b0 + 3 * bkw, bkw), act="silu"))
    o_b = _mixer_b(q_b, k_b, v_b, g_b, norm_b, hb)
    return _matmul(o_a.reshape(b * t, akw), w_out.astype(BF16), a2=o_b.reshape(b * t, bkw), res=x2d,
                   emit_stats=True)


def _odd_layer(x2d, xb, ssq, b, t, gain, w_in, w_gk_down, w_gk_up, b_gk, norm_c, w_out):
    d = x2d.shape[1]
    kw, vw = d // 2, d
    dk = kw // C_HEADS

    def seq(a2d):
        return a2d.reshape(b, t, -1)

    gcol = gain.astype(F32)[:, None]
    w_bf = (w_in * gcol).astype(BF16)
    q = seq(_matmul(xb, w_bf, cols=(0, kw), row_ssq=ssq, act=dk ** -0.5))
    k = seq(_matmul(xb, w_bf, cols=(kw, kw), row_ssq=ssq))
    v = seq(_matmul(xb, w_bf, cols=(2 * kw, vw), row_ssq=ssq))
    r = seq(_matmul(xb, w_bf, cols=(2 * kw + vw, vw), row_ssq=ssq, act="silu"))
    lr = seq(_matmul(xb, _pad_cols(w_gk_down * gcol, LANES).astype(BF16), row_ssq=ssq))
    w_up = jnp.pad(w_gk_up, ((0, LANES - w_gk_up.shape[0]), (0, 0))).astype(BF16)
    o_c = _mixer_c(q, k, v, r, lr, w_up, b_gk, norm_c).reshape(b * t, vw)
    return _matmul(o_c, w_out.astype(BF16), res=x2d, emit_stats=True)


def kernel(x, norm_mix, norm_mlp, norm_final, w_up, w_down, w_in_ab, conv_a, a_log, dt_bias, norm_a, norm_b,
           w_out_ab, w_in_c, w_gk_down, w_gk_up, b_gk, norm_c, w_out_c):
    b, t, d = x.shape
    depth = norm_mix.shape[0]
    x2d = x.reshape(b * t, d)
    w_up_g = (w_up * norm_mlp.astype(F32)[:, :, None]).astype(BF16)
    w_down = w_down.astype(BF16)
    xb = ssq = None
    for layer in range(depth):
        i = layer // 2
        if layer % 2 == 0:
            x2d, xb, ssq = _even_layer(x2d, b, t, norm_mix[layer], w_in_ab[i], conv_a[i], a_log[i], dt_bias[i],
                                       norm_a[i], norm_b[i], w_out_ab[i])
        else:
            x2d, xb, ssq = _odd_layer(x2d, xb, ssq, b, t, norm_mix[layer], w_in_c[i], w_gk_down[i], w_gk_up[i],
                                      b_gk[i], norm_c[i], w_out_c[i])
        if layer + 1 < depth:
            x2d, xb, ssq = _mlp(x2d, xb, ssq, w_up_g, w_down, layer, emit_stats=True)
        else:
            x2d = _mlp(x2d, xb, ssq, w_up_g, w_down, layer, emit_stats=False)
    return _rmsnorm(x2d, norm_final, F32).reshape(b, t, d)
```
